```python
import jax, jax.numpy as jnp
from jax import lax
import numpy as np

D_MODEL = 2048
BATCH = 8
SEQ = 4096
DEPTH = 4
DEC_BATCH = 32
DEC_SEQ = 16
PAST_LEN = 1024

CHUNK = 64
PLE_DIM = 256
NORM_EPS = 1e-6
NEG_INF = -1e30

POOL_WIDTH = D_MODEL // 2
POOL_WINDOWS = (2, 4, 8, 16)
POOL_GROUPS = len(POOL_WINDOWS)
POOL_GROUP_DIM = POOL_WIDTH // POOL_GROUPS
POOL_HIST = max(POOL_WINDOWS) - 1
CONV_WIDTH = D_MODEL // 2
CONV_KERNEL = 31
CONV_HIST = CONV_KERNEL - 1
CP_IN = POOL_WIDTH + 2 * CONV_WIDTH
CP_MIX = POOL_WIDTH + CONV_WIDTH
N_CP_LAYERS = (DEPTH + 1) // 2

N_HEADS = 16
HEAD_DIM = 128
ATT_WIDTH = N_HEADS * HEAD_DIM
FOX_IN = 3 * ATT_WIDTH + N_HEADS
Q_BLOCK = 128
N_FOX_LAYERS = DEPTH // 2

N_GROUPS = 4
EXPERTS_PER_GROUP = 8
N_EXPERTS = N_GROUPS * EXPERTS_PER_GROUP
TOP_K = 2
D_EXPERT = 512
DISPATCH_ROWS = 256

kernel_name = 'hybrid_pool_conv_fox_hmoe_stream_step'


def rmsnorm(x, g):
    xf = x.astype(jnp.float32)
    y = xf * lax.rsqrt(jnp.mean(xf * xf, axis=-1, keepdims=True) + NORM_EPS)
    return (y * g.astype(jnp.float32)).astype(x.dtype)


def layernorm(x, g, b):
    xf = x.astype(jnp.float32)
    mu = jnp.mean(xf, axis=-1, keepdims=True)
    var = jnp.mean(jnp.square(xf - mu), axis=-1, keepdims=True)
    y = (xf - mu) * lax.rsqrt(var + NORM_EPS) * g.astype(jnp.float32) + b.astype(jnp.float32)
    return y.astype(x.dtype)


def pool_conv_mixer(h, pool_hist, conv_hist, start_pos, w_in, pool_w, pool_scale,
                    conv_w, conv_b, ln_g, ln_b, w_out):
    B, T, _ = h.shape
    u = h @ w_in
    u_pool = u[..., :POOL_WIDTH]
    glu = u[..., POOL_WIDTH:POOL_WIDTH + CONV_WIDTH] * jax.nn.sigmoid(u[..., POOL_WIDTH + CONV_WIDTH:])

    ext_p = jnp.concatenate([pool_hist, u_pool], axis=1)
    cs = jnp.pad(jnp.cumsum(ext_p.astype(jnp.float32), axis=1), ((0, 0), (1, 0), (0, 0)))
    groups = []
    for g, w in enumerate(POOL_WINDOWS):
        cols = slice(g * POOL_GROUP_DIM, (g + 1) * POOL_GROUP_DIM)
        win_sum = cs[:, POOL_HIST + 1:, cols] - cs[:, POOL_HIST + 1 - w:POOL_HIST + 1 - w + T, cols]
        count = jnp.minimum(start_pos + jnp.arange(T) + 1, w).astype(jnp.float32)[None, :, None]
        groups.append(win_sum / count)
    pooled = jnp.stack(groups, axis=2) - u_pool.astype(jnp.float32).reshape(B, T, POOL_GROUPS, POOL_GROUP_DIM)
    pool_out = jnp.einsum('btgc,gcd->btgd', pooled.astype(h.dtype), pool_w).reshape(B, T, POOL_WIDTH) * pool_scale

    ext_c = jnp.concatenate([conv_hist, glu], axis=1)
    conv = lax.conv_general_dilated(ext_c, conv_w[:, None, :], window_strides=(1,), padding='VALID',
                                    dimension_numbers=('NWC', 'WIO', 'NWC'),
                                    feature_group_count=CONV_WIDTH) + conv_b
    conv_out = jax.nn.silu(layernorm(conv, ln_g, ln_b))

    y = jnp.concatenate([pool_out, conv_out], axis=-1) @ w_out
    return y, ext_p[:, -POOL_HIST:], ext_c[:, -CONV_HIST:]


def fox_project(h, w_in, b_f, g_q, g_k):
    B, T, _ = h.shape
    u = h @ w_in
    q = rmsnorm(u[..., :ATT_WIDTH].reshape(B, T, N_HEADS, HEAD_DIM), g_q)
    k = rmsnorm(u[..., ATT_WIDTH:2 * ATT_WIDTH].reshape(B, T, N_HEADS, HEAD_DIM), g_k)
    v = u[..., 2 * ATT_WIDTH:3 * ATT_WIDTH].reshape(B, T, N_HEADS, HEAD_DIM)
    logf = jax.nn.log_sigmoid(u[..., 3 * ATT_WIDTH:].astype(jnp.float32) + b_f.astype(jnp.float32))
    return q, k, v, logf


def fox_probs(q, k, f_q, f_k, mask):
    s = jnp.einsum('bqhd,bkhd->bhqk', q, k).astype(jnp.float32) * (HEAD_DIM ** -0.5)
    s = s + f_q[..., :, None] - f_k[..., None, :]
    return jax.nn.softmax(jnp.where(mask, s, NEG_INF), axis=-1)


def fox_prompt(h, w_in, b_f, g_q, g_k, w_out):
    B, S, _ = h.shape
    q, k, v, logf = fox_project(h, w_in, b_f, g_q, g_k)
    f_cum = jnp.cumsum(logf, axis=1).transpose(0, 2, 1)
    n_blk = S // Q_BLOCK
    q_blk = q.reshape(B, n_blk, Q_BLOCK, N_HEADS, HEAD_DIM).swapaxes(0, 1)
    f_blk = f_cum.reshape(B, N_HEADS, n_blk, Q_BLOCK).transpose(2, 0, 1, 3)
    q_pos = jnp.arange(S).reshape(n_blk, Q_BLOCK)
    k_pos = jnp.arange(S)

    def block(args):
        q_i, f_i, pos_i = args
        p = fox_probs(q_i, k, f_i, f_cum, pos_i[:, None] >= k_pos[None, :])
        return jnp.einsum('bhqk,bkhd->bqhd', p.astype(v.dtype), v)

    o = lax.map(block, (q_blk, f_blk, q_pos)).swapaxes(0, 1).reshape(B, S, ATT_WIDTH)
    return o @ w_out, k, v, logf.astype(h.dtype)


def fox_sample(h, past_k, past_v, past_logf, w_in, b_f, g_q, g_k, w_out):
    B, T, _ = h.shape
    P = past_k.shape[1]
    q, k, v, logf = fox_project(h, w_in, b_f, g_q, g_k)
    k_all = jnp.concatenate([past_k, k], axis=1)
    v_all = jnp.concatenate([past_v, v], axis=1)
    f_cum = jnp.cumsum(jnp.concatenate([past_logf.astype(jnp.float32), logf], axis=1), axis=1).transpose(0, 2, 1)
    mask = (P + jnp.arange(T))[:, None] >= jnp.arange(P + T)[None, :]
    p = fox_probs(q, k_all, f_cum[..., P:], f_cum, mask)
    o = jnp.einsum('bhqk,bkhd->bqhd', p.astype(v_all.dtype), v_all).reshape(B, T, ATT_WIDTH)
    return o @ w_out, k, v, logf.astype(h.dtype)


def expert_mlp(rows, w_gate, w_up, w_down):
    return (jax.nn.silu(rows @ w_gate) * (rows @ w_up)) @ w_down


def routed_experts(ht, expert, weight, w_gate, w_up, w_down):
    T = ht.shape[0]
    A = T * TOP_K
    flat_e = expert.reshape(A)
    order = jnp.argsort(flat_e)
    sorted_e = flat_e[order]
    token = order // TOP_K
    counts = jnp.bincount(flat_e, length=N_EXPERTS)
    padded = (counts + DISPATCH_ROWS - 1) // DISPATCH_ROWS * DISPATCH_ROWS
    pad_end = jnp.cumsum(padded)
    pad_start = pad_end - padded
    start = jnp.cumsum(counts) - counts
    dest = pad_start[sorted_e] + jnp.arange(A) - start[sorted_e]
    n_blocks = -(-A // DISPATCH_ROWS) + N_EXPERTS
    buf = jnp.zeros((n_blocks * DISPATCH_ROWS, ht.shape[1]), ht.dtype).at[dest].set(ht[token])
    block_expert = jnp.minimum(jnp.searchsorted(pad_end, jnp.arange(n_blocks) * DISPATCH_ROWS, side='right'),
                               N_EXPERTS - 1)

    def run(args):
        rows, e = args
        return expert_mlp(rows, w_gate[e], w_up[e], w_down[e])

    out = lax.map(run, (buf.reshape(n_blocks, DISPATCH_ROWS, -1), block_expert)).reshape(n_blocks * DISPATCH_ROWS, -1)
    contrib = out[dest] * weight.reshape(A)[order][:, None].astype(ht.dtype)
    return jnp.zeros_like(ht).at[token].add(contrib)


def hier_moe(h, w_rg, b_rg, w_re, b_re, w_gate, w_up, w_down):
    ht = h.reshape(-1, D_MODEL)
    T = ht.shape[0]
    g_logits = (ht @ w_rg).astype(jnp.float32) + b_rg.astype(jnp.float32)
    g_prob = jax.nn.softmax(g_logits, axis=-1)
    g_sel = jnp.argmax(g_logits, axis=-1)
    e_logits = ((ht @ w_re).astype(jnp.float32) + b_re.astype(jnp.float32)).reshape(T, N_GROUPS, EXPERTS_PER_GROUP)
    e_in_group = jnp.take_along_axis(e_logits, g_sel[:, None, None], axis=1)[:, 0]
    top_val, top_idx = lax.top_k(e_in_group, TOP_K)
    weight = jax.nn.softmax(top_val, axis=-1) * jnp.take_along_axis(g_prob, g_sel[:, None], axis=1)
    expert = g_sel[:, None] * EXPERTS_PER_GROUP + top_idx
    return routed_experts(ht, expert, weight, w_gate, w_up, w_down).reshape(h.shape)


def per_layer_embedding(x, p, g_norm, w_gate, w_proj):
    gate = jax.nn.sigmoid(rmsnorm(x, g_norm) @ w_gate)
    return gate * (p @ w_proj)


def setup_inputs(seed: int = 0) -> dict:
    key = jax.random.key(seed)
    keys = iter(jax.random.split(key, 64))

    def nrm(shape, scale):
        return jax.random.normal(next(keys), shape, jnp.float32) * scale

    def gain(shape):
        return 1.0 + nrm(shape, 0.05)

    forget_bias = jnp.linspace(1.0, 6.0, N_HEADS, dtype=jnp.float32)
    return {
        'x_prompt': nrm((BATCH, SEQ, D_MODEL), 1.0),
        'x_sample': nrm((DEC_BATCH, DEC_SEQ, D_MODEL), 1.0),
        'cache_k': nrm((N_FOX_LAYERS, DEC_BATCH, PAST_LEN, N_HEADS, HEAD_DIM), 1.0),
        'cache_v': nrm((N_FOX_LAYERS, DEC_BATCH, PAST_LEN, N_HEADS, HEAD_DIM), 1.0),
        'cache_logf': jax.nn.log_sigmoid(forget_bias + nrm((N_FOX_LAYERS, DEC_BATCH, PAST_LEN, N_HEADS), 1.0)),
        'state_pool': nrm((N_CP_LAYERS, DEC_BATCH, POOL_HIST, POOL_WIDTH), 1.0),
        'state_conv': nrm((N_CP_LAYERS, DEC_BATCH, CONV_HIST, CONV_WIDTH), 0.5),
        'p_prompt': nrm((DEPTH, BATCH, SEQ, PLE_DIM), 1.0),
        'p_sample': nrm((DEPTH, DEC_BATCH, DEC_SEQ, PLE_DIM), 1.0),
        'norm_mix': gain((DEPTH, D_MODEL)),
        'norm_ffn': gain((DEPTH, D_MODEL)),
        'norm_ple': gain((DEPTH, D_MODEL)),
        'w_in_cp': nrm((N_CP_LAYERS, D_MODEL, CP_IN), D_MODEL ** -0.5),
        'pool_w': nrm((N_CP_LAYERS, POOL_GROUPS, POOL_GROUP_DIM, POOL_GROUP_DIM), POOL_GROUP_DIM ** -0.5),
        'pool_scale': gain((N_CP_LAYERS, POOL_WIDTH)),
        'conv_w': nrm((N_CP_LAYERS, CONV_KERNEL, CONV_WIDTH), CONV_KERNEL ** -0.5),
        'conv_b': nrm((N_CP_LAYERS, CONV_WIDTH), 0.01),
        'conv_ln_g': gain((N_CP_LAYERS, CONV_WIDTH)),
        'conv_ln_b': nrm((N_CP_LAYERS, CONV_WIDTH), 0.01),
        'w_out_cp': nrm((N_CP_LAYERS, CP_MIX, D_MODEL), CP_MIX ** -0.5),
        'w_in_fox': nrm((N_FOX_LAYERS, D_MODEL, FOX_IN), D_MODEL ** -0.5),
        'b_forget': forget_bias[None, :] + nrm((N_FOX_LAYERS, N_HEADS), 0.01),
        'q_norm': gain((N_FOX_LAYERS, HEAD_DIM)),
        'k_norm': gain((N_FOX_LAYERS, HEAD_DIM)),
        'w_out_fox': nrm((N_FOX_LAYERS, ATT_WIDTH, D_MODEL), ATT_WIDTH ** -0.5),
        'router_group_w': nrm((DEPTH, D_MODEL, N_GROUPS), D_MODEL ** -0.5),
        'router_group_b': nrm((DEPTH, N_GROUPS), 0.01),
        'router_expert_w': nrm((DEPTH, D_MODEL, N_EXPERTS), D_MODEL ** -0.5),
        'router_expert_b': nrm((DEPTH, N_EXPERTS), 0.01),
        'expert_w_gate': nrm((DEPTH, N_EXPERTS, D_MODEL, D_EXPERT), D_MODEL ** -0.5),
        'expert_w_up': nrm((DEPTH, N_EXPERTS, D_MODEL, D_EXPERT), D_MODEL ** -0.5),
        'expert_w_down': nrm((DEPTH, N_EXPERTS, D_EXPERT, D_MODEL), D_EXPERT ** -0.5),
        'w_ple_gate': nrm((DEPTH, D_MODEL, D_MODEL), D_MODEL ** -0.5),
        'w_ple_proj': nrm((DEPTH, PLE_DIM, D_MODEL), PLE_DIM ** -0.5),
    }


def reference(x_prompt, x_sample, cache_k, cache_v, cache_logf, state_pool, state_conv, p_prompt, p_sample,
              norm_mix, norm_ffn, norm_ple, w_in_cp, pool_w, pool_scale, conv_w, conv_b, conv_ln_g, conv_ln_b,
              w_out_cp, w_in_fox, b_forget, q_norm, k_norm, w_out_fox, router_group_w, router_group_b,
              router_expert_w, router_expert_b, expert_w_gate, expert_w_up, expert_w_down, w_ple_gate, w_ple_proj):
    xp, xs = x_prompt, x_sample
    bp = xp.shape[0]
    k_p, v_p, lf_p, pool_p, conv_p = [], [], [], [], []
    k_s, v_s, lf_s, pool_s, conv_s = [], [], [], [], []
    for i in range(DEPTH):
        j = i // 2
        hp = rmsnorm(xp, norm_mix[i])
        hs = rmsnorm(xs, norm_mix[i])
        if i % 2 == 0:
            zero_pool = jnp.zeros((bp, POOL_HIST, POOL_WIDTH), xp.dtype)
            zero_conv = jnp.zeros((bp, CONV_HIST, CONV_WIDTH), xp.dtype)
            dp, np_pool, np_conv = pool_conv_mixer(hp, zero_pool, zero_conv, 0, w_in_cp[j], pool_w[j],
                                                   pool_scale[j], conv_w[j], conv_b[j], conv_ln_g[j],
                                                   conv_ln_b[j], w_out_cp[j])
            ds, ns_pool, ns_conv = pool_conv_mixer(hs, state_pool[j], state_conv[j], PAST_LEN, w_in_cp[j],
                                                   pool_w[j], pool_scale[j], conv_w[j], conv_b[j],
                                                   conv_ln_g[j], conv_ln_b[j], w_out_cp[j])
            pool_p.append(np_pool)
            conv_p.append(np_conv)
            pool_s.append(ns_pool)
            conv_s.append(ns_conv)
        else:
            dp, kp_new, vp_new, lfp_new = fox_prompt(hp, w_in_fox[j], b_forget[j], q_norm[j], k_norm[j],
                                                     w_out_fox[j])
            ds, ks_new, vs_new, lfs_new = fox_sample(hs, cache_k[j], cache_v[j], cache_logf[j], w_in_fox[j],
                                                     b_forget[j], q_norm[j], k_norm[j], w_out_fox[j])
            k_p.append(kp_new)
            v_p.append(vp_new)
            lf_p.append(lfp_new)
            k_s.append(ks_new)
            v_s.append(vs_new)
            lf_s.append(lfs_new)
        xp = xp + dp
        xs = xs + ds
        xp = xp + hier_moe(rmsnorm(xp, norm_ffn[i]), router_group_w[i], router_group_b[i], router_expert_w[i],
                           router_expert_b[i], expert_w_gate[i], expert_w_up[i], expert_w_down[i])
        xs = xs + hier_moe(rmsnorm(xs, norm_ffn[i]), router_group_w[i], router_group_b[i], router_expert_w[i],
                           router_expert_b[i], expert_w_gate[i], expert_w_up[i], expert_w_down[i])
        xp = xp + per_layer_embedding(xp, p_prompt[i], norm_ple[i], w_ple_gate[i], w_ple_proj[i])
        xs = xs + per_layer_embedding(xs, p_sample[i], norm_ple[i], w_ple_gate[i], w_ple_proj[i])
    return (xp, xs, jnp.stack(k_p), jnp.stack(v_p), jnp.stack(lf_p), jnp.stack(pool_p), jnp.stack(conv_p),
            jnp.stack(k_s), jnp.stack(v_s), jnp.stack(lf_s), jnp.stack(pool_s), jnp.stack(conv_s))
```

```python
import functools

import jax
import jax.numpy as jnp
from jax import lax
from jax.experimental import pallas as pl
from jax.experimental.pallas import tpu as pltpu

F32 = jnp.float32
BF16 = jnp.bfloat16

NORM_EPS = 1e-6
NEG_INF = -1e30
POOL_WINDOWS = (2, 4, 8, 16)
TOP_K = 2

V7X_VMEM_BYTES = 64 * 1024 * 1024
VMEM_LIMIT = V7X_VMEM_BYTES - 8 * 1024 * 1024
LANES = 128
SUBLANES = 8


def _cparams(*sem):
    return pltpu.CompilerParams(dimension_semantics=sem, vmem_limit_bytes=VMEM_LIMIT)


def _pick(n, prefs):
    for p in prefs:
        if n % p == 0:
            return p
    return n


def _rms_rows(x, g):
    ms = jnp.mean(x * x, axis=-1, keepdims=True)
    return x * lax.rsqrt(ms + NORM_EPS) * g


def _fill_normed(x_ref, g_ref, h_ref):
    @pl.when(pl.program_id(1) == 0)
    def _():
        h_ref[...] = _rms_rows(x_ref[...], g_ref[...]).astype(h_ref.dtype)


def _nm_plain_kernel(x_ref, g_ref, w_ref, o_ref, h_ref):
    _fill_normed(x_ref, g_ref, h_ref)
    o_ref[...] = jnp.dot(h_ref[...], w_ref[...], preferred_element_type=F32).astype(o_ref.dtype)


def _nm_head_kernel(x_ref, g_ref, w_ref, hg_ref, *rest, head_dim, scale, normed, n_out):
    out_refs, h_ref = rest[:n_out], rest[n_out]
    _fill_normed(x_ref, g_ref, h_ref)
    acc = jnp.dot(h_ref[...], w_ref[...], preferred_element_type=F32)
    for c0 in range(0, acc.shape[1], head_dim):
        a = acc[:, c0:c0 + head_dim]
        if normed:
            a = _rms_rows(a, hg_ref[...])
        for o_ref in out_refs:
            if o_ref.dtype == BF16:
                o_ref[:, c0:c0 + head_dim] = (a * scale).astype(BF16)
            else:
                o_ref[:, c0:c0 + head_dim] = a


def _nm_gate_kernel(x_ref, g_ref, w_ref, b_ref, o_ref, h_ref):
    _fill_normed(x_ref, g_ref, h_ref)
    z = jnp.dot(h_ref[...], w_ref[...], preferred_element_type=F32) + b_ref[...]
    o_ref[...] = jnp.minimum(z, 0.0) - jnp.log1p(jnp.exp(-jnp.abs(z)))


def _nm_ple_kernel(x_ref, g_ref, w_ref, p_ref, wp_ref, o_ref, h_ref, *, tn):
    _fill_normed(x_ref, g_ref, h_ref)
    j = pl.program_id(1)
    gate = jax.nn.sigmoid(jnp.dot(h_ref[...], w_ref[...], preferred_element_type=F32))
    proj = jnp.dot(p_ref[...].astype(BF16), wp_ref[...], preferred_element_type=F32)
    xres = x_ref[:, pl.ds(pl.multiple_of(j * tn, tn), tn)]
    o_ref[...] = xres + gate * proj


def _norm_matmul(kernel, x, g, w, extra_in, extra_specs, out_shapes, out_specs, tm, tn):
    m, d = x.shape
    n = w.shape[1]
    grid = (m // tm, n // tn)
    in_specs = [
        pl.BlockSpec((tm, d), lambda i, j: (i, 0)),
        pl.BlockSpec((1, d), lambda i, j: (0, 0)),
        pl.BlockSpec((d, tn), lambda i, j: (0, j)),
    ] + extra_specs
    return pl.pallas_call(
        kernel,
        grid=grid,
        in_specs=in_specs,
        out_specs=out_specs,
        out_shape=out_shapes,
        scratch_shapes=[pltpu.VMEM((tm, d), BF16)],
        compiler_params=_cparams("parallel", "arbitrary"),
    )(x, g.reshape(1, d), w, *extra_in)


def _row_tile(m):
    return _pick(m, (512, 256, 128, 64, 32, 16, 8))


def norm_matmul_plain(x, g, w, out_dtype):
    m, _ = x.shape
    n = w.shape[1]
    tm, tn = _row_tile(m), _pick(n, (512, 256, 128))
    return _norm_matmul(
        _nm_plain_kernel, x, g, w, [], [],
        jax.ShapeDtypeStruct((m, n), out_dtype),
        pl.BlockSpec((tm, tn), lambda i, j: (i, j)), tm, tn)


def norm_matmul_heads(x, g, w, head_gain, *, head_dim, scale, normed, out_dtypes):
    m, _ = x.shape
    n = w.shape[1]
    tm, tn = _row_tile(m), _pick(n, (512, 256, 128))
    kern = functools.partial(_nm_head_kernel, head_dim=head_dim, scale=scale, normed=normed,
                             n_out=len(out_dtypes))
    return _norm_matmul(
        kern, x, g, w, [head_gain.reshape(1, head_dim)],
        [pl.BlockSpec((1, head_dim), lambda i, j: (0, 0))],
        [jax.ShapeDtypeStruct((m, n), dt) for dt in out_dtypes],
        [pl.BlockSpec((tm, tn), lambda i, j: (i, j)) for _ in out_dtypes], tm, tn)


def norm_matmul_gate(x, g, w_pad, b_pad):
    m, _ = x.shape
    n = w_pad.shape[1]
    tm = _row_tile(m)
    return _norm_matmul(
        _nm_gate_kernel, x, g, w_pad, [b_pad.reshape(1, n)],
        [pl.BlockSpec((1, n), lambda i, j: (0, 0))],
        jax.ShapeDtypeStruct((m, n), F32),
        pl.BlockSpec((tm, n), lambda i, j: (i, 0)), tm, n)


def norm_matmul_ple(x, g, w, p, wp):
    m, d = x.shape
    n = w.shape[1]
    pd = p.shape[1]
    tm, tn = _row_tile(m), _pick(n, (512, 256, 128))
    kern = functools.partial(_nm_ple_kernel, tn=tn)
    return _norm_matmul(
        kern, x, g, w, [p, wp],
        [pl.BlockSpec((tm, pd), lambda i, j: (i, 0)),
         pl.BlockSpec((pd, tn), lambda i, j: (0, j))],
        jax.ShapeDtypeStruct((m, n), F32),
        pl.BlockSpec((tm, tn), lambda i, j: (i, j)), tm, tn)


def _mm_res_kernel(a_ref, w_ref, x_ref, o_ref):
    o_ref[...] = x_ref[...] + jnp.dot(a_ref[...], w_ref[...], preferred_element_type=F32)


def matmul_residual(a, w, x):
    m, k = a.shape
    n = w.shape[1]
    tm, tn = _row_tile(m), _pick(n, (512, 256, 128))
    return pl.pallas_call(
        _mm_res_kernel,
        grid=(m // tm, n // tn),
        in_specs=[
            pl.BlockSpec((tm, k), lambda i, j: (i, 0)),
            pl.BlockSpec((k, tn), lambda i, j: (0, j)),
            pl.BlockSpec((tm, tn), lambda i, j: (i, j)),
        ],
        out_specs=pl.BlockSpec((tm, tn), lambda i, j: (i, j)),
        out_shape=jax.ShapeDtypeStruct((m, n), F32),
        compiler_params=_cparams("parallel", "arbitrary"),
    )(a, w, x)


POOL_HIST_ROWS = 16
CONV_HIST_ROWS = 32


def _cp_mixer_kernel(u_ref, hp_ref, hc_ref, pw_ref, ps_ref, cw_ref, cb_ref, lg_ref, lb_ref,
                     mix_ref, npool_ref, nconv_ref, extp_ref, extc_ref, conv_ref,
                     *, tt, width, start_pos, conv_k, row_chunk, col_chunk):
    ti = pl.program_id(1)
    n_t = pl.num_programs(1)
    ph, ch = POOL_HIST_ROWS, CONV_HIST_ROWS
    w = width
    gd = w // len(POOL_WINDOWS)

    @pl.when(ti == 0)
    def _():
        extp_ref[0:ph, :] = hp_ref[0]
        extc_ref[0:ch, :] = hc_ref[0]

    @pl.when(ti > 0)
    def _():
        extp_ref[0:ph, :] = extp_ref[tt:tt + ph, :]
        extc_ref[0:ch, :] = extc_ref[tt:tt + ch, :]

    extp_ref[ph:ph + tt, :] = u_ref[0, :, 0:w].astype(F32)
    ua = u_ref[0, :, w:2 * w].astype(F32)
    ub = u_ref[0, :, 2 * w:3 * w].astype(F32)
    extc_ref[ch:ch + tt, :] = ua * jax.nn.sigmoid(ub)

    for r0 in range(0, tt, row_chunk):
        pos = (start_pos + 1 + r0 + ti * tt
               + lax.broadcasted_iota(jnp.int32, (row_chunk, 1), 0)).astype(F32)
        for g, win in enumerate(POOL_WINDOWS):
            c0 = g * gd
            cur = extp_ref[ph + r0:ph + r0 + row_chunk, c0:c0 + gd]
            acc = cur
            for j in range(1, win):
                acc = acc + extp_ref[ph + r0 - j:ph + r0 - j + row_chunk, c0:c0 + gd]
            pooled = acc / jnp.minimum(pos, float(win)) - cur
            po = jnp.dot(pooled.astype(BF16), pw_ref[g], preferred_element_type=F32)
            mix_ref[0, r0:r0 + row_chunk, c0:c0 + gd] = (po * ps_ref[:, c0:c0 + gd]).astype(BF16)

    base = ch - (conv_k - 1)
    for r0 in range(0, tt, row_chunk):
        for c0 in range(0, w, col_chunk):
            acc = jnp.broadcast_to(cb_ref[:, c0:c0 + col_chunk], (row_chunk, col_chunk))
            for k in range(conv_k):
                acc = acc + (cw_ref[k:k + 1, c0:c0 + col_chunk]
                             * extc_ref[base + r0 + k:base + r0 + k + row_chunk, c0:c0 + col_chunk])
            conv_ref[r0:r0 + row_chunk, c0:c0 + col_chunk] = acc

    for r0 in range(0, tt, row_chunk):
        c = conv_ref[r0:r0 + row_chunk, :]
        mu = jnp.mean(c, axis=-1, keepdims=True)
        cc = c - mu
        var = jnp.mean(cc * cc, axis=-1, keepdims=True)
        y = cc * lax.rsqrt(var + NORM_EPS) * lg_ref[...] + lb_ref[...]
        mix_ref[0, r0:r0 + row_chunk, w:2 * w] = (y * jax.nn.sigmoid(y)).astype(BF16)

    @pl.when(ti == n_t - 1)
    def _():
        npool_ref[0] = extp_ref[ph + tt - (ph - 1):ph + tt, :]
        nconv_ref[0] = extc_ref[ch + tt - (conv_k - 1):ch + tt, :]


def cp_mixer(u, hist_pool, hist_conv, pool_w, pool_scale, conv_w, conv_b, ln_g, ln_b, start_pos):
    b, t, w3 = u.shape
    w = w3 // 3
    conv_k = conv_w.shape[0]
    n_pool_hist = hist_pool.shape[1]
    assert n_pool_hist == POOL_HIST_ROWS - 1 and conv_k - 1 <= CONV_HIST_ROWS
    tt = _pick(t, (256, 128, 64, 32, 16))
    row_chunk = min(tt, 64)
    hp = jnp.pad(hist_pool, ((0, 0), (POOL_HIST_ROWS - n_pool_hist, 0), (0, 0)))
    hc = jnp.pad(hist_conv, ((0, 0), (CONV_HIST_ROWS - (conv_k - 1), 0), (0, 0)))
    kern = functools.partial(_cp_mixer_kernel, tt=tt, width=w, start_pos=start_pos, conv_k=conv_k,
                             row_chunk=row_chunk, col_chunk=2 * LANES)
    n_g, gd = pool_w.shape[0], pool_w.shape[1]
    vec = lambda: pl.BlockSpec((1, w), lambda i, j: (0, 0))
    return pl.pallas_call(
        kern,
        grid=(b, t // tt),
        in_specs=[
            pl.BlockSpec((1, tt, w3), lambda i, j: (i, j, 0)),
            pl.BlockSpec((1, POOL_HIST_ROWS, w), lambda i, j: (i, 0, 0)),
            pl.BlockSpec((1, CONV_HIST_ROWS, w), lambda i, j: (i, 0, 0)),
            pl.BlockSpec((n_g, gd, gd), lambda i, j: (0, 0, 0)),
            vec(),
            pl.BlockSpec((conv_k, w), lambda i, j: (0, 0)),
            vec(), vec(), vec(),
        ],
        out_specs=[
            pl.BlockSpec((1, tt, 2 * w), lambda i, j: (i, j, 0)),
            pl.BlockSpec((1, n_pool_hist, w), lambda i, j: (i, 0, 0)),
            pl.BlockSpec((1, conv_k - 1, w), lambda i, j: (i, 0, 0)),
        ],
        out_shape=[
            jax.ShapeDtypeStruct((b, t, 2 * w), BF16),
            jax.ShapeDtypeStruct((b, n_pool_hist, w), F32),
            jax.ShapeDtypeStruct((b, conv_k - 1, w), F32),
        ],
        scratch_shapes=[
            pltpu.VMEM((POOL_HIST_ROWS + tt, w), F32),
            pltpu.VMEM((CONV_HIST_ROWS + tt, w), F32),
            pltpu.VMEM((tt, w), F32),
        ],
        compiler_params=_cparams("arbitrary", "arbitrary"),
    )(u, hp, hc, pool_w.astype(BF16), pool_scale.reshape(1, w), conv_w, conv_b.reshape(1, w),
      ln_g.reshape(1, w), ln_b.reshape(1, w))


def _softmax_step(s, m, l, acc, v):
    m_new = jnp.maximum(m, jnp.max(s, axis=-1, keepdims=True))
    alpha = jnp.exp(m - m_new)
    p = jnp.exp(s - m_new)
    l_new = alpha * l + jnp.sum(p, axis=-1, keepdims=True)
    acc_new = alpha * acc + jnp.dot(p.astype(BF16), v, preferred_element_type=F32)
    return m_new, l_new, acc_new


_NT = (((1,), (1,)), ((), ()))


def _flash_kernel(q_ref, k_ref, v_ref, f_ref, o_ref, *, tq, n_q, head_dim):
    def q_block(i, _):
        q0 = pl.multiple_of(i * tq, tq)
        q = q_ref[0, pl.ds(q0, tq), :]
        f_row = f_ref[0, 0, :, pl.ds(q0, tq)]
        f_q = jnp.transpose(jnp.broadcast_to(f_row, (LANES, tq)))[:, 0:1]

        def scores(j):
            k0 = pl.multiple_of(j * tq, tq)
            k = k_ref[0, pl.ds(k0, tq), :]
            v = v_ref[0, pl.ds(k0, tq), :]
            f_k = f_ref[0, 0, :, pl.ds(k0, tq)]
            s = lax.dot_general(q, k, _NT, preferred_element_type=F32) + (f_q - f_k)
            return s, v

        def k_block(j, carry):
            s, v = scores(j)
            return _softmax_step(s, *carry, v)

        init = (jnp.full((tq, 1), NEG_INF, F32), jnp.zeros((tq, 1), F32),
                jnp.zeros((tq, head_dim), F32))
        carry = lax.fori_loop(0, i, k_block, init)
        s, v = scores(i)
        causal = (lax.broadcasted_iota(jnp.int32, (tq, tq), 0)
                  >= lax.broadcasted_iota(jnp.int32, (tq, tq), 1))
        _, l, acc = _softmax_step(jnp.where(causal, s, NEG_INF), *carry, v)
        o_ref[0, pl.ds(q0, tq), :] = (acc / l).astype(o_ref.dtype)
        return 0

    lax.fori_loop(0, n_q, q_block, 0)


def flash_prompt(q, k, v, f_row, *, n_heads, head_dim):
    b, s, hd = q.shape
    tq = _pick(s, (256, 128))
    kern = functools.partial(_flash_kernel, tq=tq, n_q=s // tq, head_dim=head_dim)
    head = lambda: pl.BlockSpec((1, s, head_dim), lambda i, h: (i, 0, h))
    return pl.pallas_call(
        kern,
        grid=(b, n_heads),
        in_specs=[head(), head(), head(), pl.BlockSpec((1, 1, 1, s), lambda i, h: (i, h, 0, 0))],
        out_specs=head(),
        out_shape=jax.ShapeDtypeStruct((b, s, hd), BF16),
        compiler_params=_cparams("parallel", "parallel"),
    )(q, k, v, f_row)


def _sample_attn_kernel(q_ref, kn_ref, vn_ref, kc_ref, vc_ref, fp_ref, fn_ref, o_ref,
                        qbd_ref, m_ref, l_ref, acc_ref, *, t, n_heads, head_dim):
    j = pl.program_id(1)
    n_j = pl.num_programs(1)
    r = n_heads * t
    hd = n_heads * head_dim

    def same_head():
        return (lax.broadcasted_iota(jnp.int32, (r, hd), 0) // t
                == lax.broadcasted_iota(jnp.int32, (r, hd), 1) // head_dim)

    def rep_rows(x):
        return jnp.broadcast_to(x[:, None, :], (n_heads, t, x.shape[-1])).reshape(r, x.shape[-1])

    f_new = rep_rows(fn_ref[0])
    row_t = lax.broadcasted_iota(jnp.int32, (r, t), 0) % t
    col_t = lax.broadcasted_iota(jnp.int32, (r, t), 1)
    f_q = jnp.sum(jnp.where(col_t == row_t, f_new, 0.0), axis=-1, keepdims=True)

    @pl.when(j == 0)
    def _():
        q_rep = jnp.broadcast_to(q_ref[0][None], (n_heads, t, hd)).reshape(r, hd)
        qbd_ref[...] = jnp.where(same_head(), q_rep, jnp.zeros_like(q_rep))
        m_ref[...] = jnp.full(m_ref.shape, NEG_INF, F32)
        l_ref[...] = jnp.zeros(l_ref.shape, F32)
        acc_ref[...] = jnp.zeros(acc_ref.shape, F32)

    def update(k, v, bias, valid):
        s = lax.dot_general(qbd_ref[...], k, _NT, preferred_element_type=F32) + bias
        if valid is not None:
            s = jnp.where(valid, s, NEG_INF)
        m, l, acc = _softmax_step(s, m_ref[...], l_ref[...], acc_ref[...], v)
        m_ref[...] = m
        l_ref[...] = l
        acc_ref[...] = acc

    update(kc_ref[0].astype(BF16), vc_ref[0].astype(BF16), f_q - rep_rows(fp_ref[0]), None)

    @pl.when(j == n_j - 1)
    def _():
        update(kn_ref[0], vn_ref[0], f_q - f_new, col_t <= row_t)
        o_full = jnp.where(same_head(), acc_ref[...] / l_ref[...], 0.0)
        o = o_full[0:t]
        for h in range(1, n_heads):
            o = o + o_full[h * t:(h + 1) * t]
        o_ref[0] = o.astype(o_ref.dtype)


def sample_attention(q, k_new, v_new, cache_k, cache_v, f_past, f_new, *, n_heads, head_dim):
    b, t, hd = q.shape
    p = cache_k.shape[1]
    tp = _pick(p, (512, 256, 128))
    r = n_heads * t
    kern = functools.partial(_sample_attn_kernel, t=t, n_heads=n_heads, head_dim=head_dim)
    new = lambda: pl.BlockSpec((1, t, hd), lambda i, j: (i, 0, 0))
    past = lambda: pl.BlockSpec((1, tp, hd), lambda i, j: (i, j, 0))
    return pl.pallas_call(
        kern,
        grid=(b, p // tp),
        in_specs=[new(), new(), new(), past(), past(),
                  pl.BlockSpec((1, n_heads, tp), lambda i, j: (i, 0, j)),
                  pl.BlockSpec((1, n_heads, t), lambda i, j: (i, 0, 0))],
        out_specs=new(),
        out_shape=jax.ShapeDtypeStruct((b, t, hd), BF16),
        scratch_shapes=[pltpu.VMEM((r, hd), BF16), pltpu.VMEM((r, 1), F32), pltpu.VMEM((r, 1), F32),
                        pltpu.VMEM((r, hd), F32)],
        compiler_params=_cparams("parallel", "arbitrary"),
    )(q, k_new, v_new, cache_k, cache_v, f_past, f_new)


def _router_kernel(x_ref, g_ref, w_ref, b_ref, ids_ref, wts_ref, *, n_groups, per_group):
    h = _rms_rows(x_ref[...], g_ref[...])
    logits = jnp.dot(h, w_ref[...], preferred_element_type=F32,
                     precision=lax.Precision.HIGHEST) + b_ref[...]
    tm, n = logits.shape
    col = lax.broadcasted_iota(jnp.int32, (tm, n), 1)
    neg = jnp.float32(-jnp.inf)

    def first_max(vals):
        top = jnp.max(vals, axis=-1, keepdims=True)
        idx = jnp.min(jnp.where(vals == top, col, n), axis=-1, keepdims=True)
        return top, idx

    gl = jnp.where(col < n_groups, logits, neg)
    g_top, g_sel = first_max(gl)
    g_prob = 1.0 / jnp.sum(jnp.exp(gl - g_top), axis=-1, keepdims=True)
    lo = n_groups + g_sel * per_group
    el = jnp.where((col >= lo) & (col < lo + per_group), logits, neg)
    v1, i1 = first_max(el)
    v2, i2 = first_max(jnp.where(col == i1, neg, el))
    e2 = jnp.exp(v2 - v1)
    w1 = g_prob / (1.0 + e2)
    w2 = g_prob * e2 / (1.0 + e2)
    ids_ref[...] = jnp.where(col == 0, i1 - n_groups, jnp.where(col == 1, i2 - n_groups, 0))
    wts_ref[...] = jnp.where(col == 0, w1, jnp.where(col == 1, w2, 0.0))


def router(x, g, w_pad, b_pad, *, n_groups, per_group):
    m, d = x.shape
    n = w_pad.shape[1]
    tm = _pick(m, (256, 128, 64, 32, 16, 8))
    kern = functools.partial(_router_kernel, n_groups=n_groups, per_group=per_group)
    return pl.pallas_call(
        kern,
        grid=(m // tm,),
        in_specs=[pl.BlockSpec((tm, d), lambda i: (i, 0)),
                  pl.BlockSpec((1, d), lambda i: (0, 0)),
                  pl.BlockSpec((d, n), lambda i: (0, 0)),
                  pl.BlockSpec((1, n), lambda i: (0, 0))],
        out_specs=[pl.BlockSpec((tm, n), lambda i: (i, 0)), pl.BlockSpec((tm, n), lambda i: (i, 0))],
        out_shape=[jax.ShapeDtypeStruct((m, n), jnp.int32), jax.ShapeDtypeStruct((m, n), F32)],
        compiler_params=_cparams("parallel"),
    )(x, g.reshape(1, d), w_pad, b_pad.reshape(1, n))


EXPERT_BLOCK_ROWS = 256


def _row_copy(src_ref, src_row, dst_ref, dst_row, sem):
    return pltpu.make_async_copy(src_ref.at[pl.ds(src_row, 1)], dst_ref.at[pl.ds(dst_row, 1)], sem)


def _dispatch_kernel(dest_ref, x_ref, buf_in_ref, buf_ref, sem, *, tm):
    del buf_in_ref

    def start(r, _):
        for k in range(TOP_K):
            _row_copy(x_ref, r, buf_ref, dest_ref[0, 0, TOP_K * r + k], sem).start()
        return 0

    def wait(r, _):
        for k in range(TOP_K):
            _row_copy(x_ref, r, buf_ref, dest_ref[0, 0, TOP_K * r + k], sem).wait()
        return 0

    lax.fori_loop(0, tm, start, 0)
    lax.fori_loop(0, tm, wait, 0)


def dispatch_rows(x, dest, buf):
    m, d = x.shape
    tm = _pick(m, (256, 128, 64, 32, 16, 8))
    dest3 = dest.reshape(m // tm, 1, tm * TOP_K)
    return pl.pallas_call(
        functools.partial(_dispatch_kernel, tm=tm),
        grid=(m // tm,),
        in_specs=[pl.BlockSpec((1, 1, tm * TOP_K), lambda i: (i, 0, 0), memory_space=pltpu.SMEM),
                  pl.BlockSpec((tm, d), lambda i: (i, 0)),
                  pl.BlockSpec(memory_space=pl.ANY)],
        out_specs=pl.BlockSpec(memory_space=pl.ANY),
        out_shape=jax.ShapeDtypeStruct(buf.shape, buf.dtype),
        scratch_shapes=[pltpu.SemaphoreType.DMA(())],
        input_output_aliases={2: 0},
        compiler_params=_cparams("arbitrary"),
    )(dest3, x, buf)


def _expert_kernel(be_ref, nu_ref, x_ref, g_ref, wg_ref, wu_ref, wd_ref, o_ref):
    del be_ref

    @pl.when(pl.program_id(0) < nu_ref[0])
    def _():
        h = _rms_rows(x_ref[...], g_ref[...]).astype(BF16)
        gate = jnp.dot(h, wg_ref[0], preferred_element_type=F32)
        up = jnp.dot(h, wu_ref[0], preferred_element_type=F32)
        act = (gate * jax.nn.sigmoid(gate) * up).astype(BF16)
        o_ref[...] = jnp.dot(act, wd_ref[0], preferred_element_type=F32)

    @pl.when(pl.program_id(0) >= nu_ref[0])
    def _():
        o_ref[...] = jnp.zeros(o_ref.shape, o_ref.dtype)


def expert_mlps(buf, g, w_gate, w_up, w_down, block_expert, n_used):
    rows, d = buf.shape
    n_e, _, de = w_gate.shape
    br = EXPERT_BLOCK_ROWS
    nb = rows // br
    blk = lambda i, be, nu: (jnp.minimum(i, nu[0] - 1), 0)
    wsel = lambda i, be, nu: (be[i], 0, 0)
    return pl.pallas_call(
        _expert_kernel,
        grid_spec=pltpu.PrefetchScalarGridSpec(
            num_scalar_prefetch=2,
            grid=(nb,),
            in_specs=[pl.BlockSpec((br, d), blk),
                      pl.BlockSpec((1, d), lambda i, be, nu: (0, 0)),
                      pl.BlockSpec((1, d, de), wsel),
                      pl.BlockSpec((1, d, de), wsel),
                      pl.BlockSpec((1, de, d), wsel)],
            out_specs=pl.BlockSpec((br, d), lambda i, be, nu: (i, 0)),
        ),
        out_shape=jax.ShapeDtypeStruct((rows, d), F32),
        compiler_params=_cparams("arbitrary"),
    )(block_expert, n_used, buf, g.reshape(1, d), w_gate, w_up, w_down)


def _combine_kernel(dest_cur_ref, dest_nxt_ref, x_ref, wts_ref, y_ref, o_ref, rows_ref, sems, *, tm):
    i = pl.program_id(0)
    n = pl.num_programs(0)

    def copies(dest_ref, slot, r):
        return [_row_copy(y_ref, dest_ref[0, 0, TOP_K * r + k], rows_ref.at[slot, k], r, sems.at[slot])
                for k in range(TOP_K)]

    def start_all(dest_ref, slot):
        def body(r, _):
            for c in copies(dest_ref, slot, r):
                c.start()
            return 0
        lax.fori_loop(0, tm, body, 0)

    @pl.when(i == 0)
    def _():
        start_all(dest_cur_ref, 0)

    @pl.when(i + 1 < n)
    def _():
        start_all(dest_nxt_ref, (i + 1) % 2)

    slot = i % 2

    def wait_body(r, _):
        for c in copies(dest_cur_ref, slot, r):
            c.wait()
        return 0
    lax.fori_loop(0, tm, wait_body, 0)

    wts = wts_ref[...]
    o_ref[...] = (x_ref[...] + wts[:, 0:1] * rows_ref[slot, 0] + wts[:, 1:2] * rows_ref[slot, 1])


def combine_rows(x, wts, y, dest):
    m, d = x.shape
    tm = _pick(m, (256, 128, 64, 32, 16, 8))
    n_t = m // tm
    dest3 = dest.reshape(n_t, 1, tm * TOP_K)
    smem = lambda f: pl.BlockSpec((1, 1, tm * TOP_K), f, memory_space=pltpu.SMEM)
    return pl.pallas_call(
        functools.partial(_combine_kernel, tm=tm),
        grid=(n_t,),
        in_specs=[smem(lambda i: (i, 0, 0)),
                  smem(lambda i: (jnp.minimum(i + 1, n_t - 1), 0, 0)),
                  pl.BlockSpec((tm, d), lambda i: (i, 0)),
                  pl.BlockSpec((tm, wts.shape[1]), lambda i: (i, 0)),
                  pl.BlockSpec(memory_space=pl.ANY)],
        out_specs=pl.BlockSpec((tm, d), lambda i: (i, 0)),
        out_shape=jax.ShapeDtypeStruct((m, d), F32),
        scratch_shapes=[pltpu.VMEM((2, TOP_K, tm, d), F32), pltpu.SemaphoreType.DMA((2,))],
        compiler_params=_cparams("arbitrary"),
    )(dest3, dest3, x, wts, y)


def _routing_tables(flat_e, n_experts, block_rows):
    a = flat_e.shape[0]
    onehot = (flat_e[:, None] == jnp.arange(n_experts, dtype=jnp.int32)[None, :]).astype(jnp.int32)
    csum = jnp.cumsum(onehot, axis=0)
    rank = jnp.sum(csum * onehot, axis=1) - 1
    counts = csum[-1]
    padded = (counts + block_rows - 1) // block_rows * block_rows
    pad_end = jnp.cumsum(padded)
    pad_start = pad_end - padded
    dest = pad_start[flat_e] + rank
    n_blocks = -(-a // block_rows) + n_experts
    block_expert = jnp.minimum(
        jnp.searchsorted(pad_end, jnp.arange(n_blocks, dtype=jnp.int32) * block_rows, side='right'),
        n_experts - 1).astype(jnp.int32)
    n_used = (pad_end[-1:] // block_rows).astype(jnp.int32)
    return dest.astype(jnp.int32), block_expert, n_used, n_blocks


def hier_moe(xs, g, w_router, b_router, w_gate, w_up, w_down, *, n_groups, per_group):
    n_experts = n_groups * per_group
    routed = [router(x, g, w_router, b_router, n_groups=n_groups, per_group=per_group) for x in xs]
    flat_e = jnp.concatenate([ids[:, :TOP_K].reshape(-1) for ids, _ in routed])
    dest, block_expert, n_used, n_blocks = _routing_tables(flat_e, n_experts, EXPERT_BLOCK_ROWS)
    d = xs[0].shape[1]
    buf = jnp.zeros((n_blocks * EXPERT_BLOCK_ROWS, d), F32)
    offs = 0
    dests = []
    for x in xs:
        n_a = x.shape[0] * TOP_K
        dests.append(dest[offs:offs + n_a])
        offs += n_a
        buf = dispatch_rows(x, dests[-1], buf)
    y = expert_mlps(buf, g, w_gate, w_up, w_down, block_expert, n_used)
    return [combine_rows(x, wts, y, dst) for x, (_, wts), dst in zip(xs, routed, dests)]


def _pad_cols(w, n):
    return jnp.pad(w, ((0, 0), (0, n - w.shape[1])))


def kernel(x_prompt, x_sample, cache_k, cache_v, cache_logf, state_pool, state_conv, p_prompt, p_sample,
           norm_mix, norm_ffn, norm_ple, w_in_cp, pool_w, pool_scale, conv_w, conv_b, conv_ln_g, conv_ln_b,
           w_out_cp, w_in_fox, b_forget, q_norm, k_norm, w_out_fox, router_group_w, router_group_b,
           router_expert_w, router_expert_b, expert_w_gate, expert_w_up, expert_w_down, w_ple_gate, w_ple_proj):
    bp, sp, d = x_prompt.shape
    bs, ts, _ = x_sample.shape
    depth = norm_mix.shape[0]
    n_heads = b_forget.shape[1]
    head_dim = q_norm.shape[1]
    att = n_heads * head_dim
    past = cache_k.shape[2]
    n_groups = router_group_w.shape[2]
    n_experts = router_expert_w.shape[2]
    per_group = n_experts // n_groups
    width = pool_w.shape[1] * pool_w.shape[2]
    pool_hist = state_pool.shape[2]
    conv_hist = state_conv.shape[2]
    scale = float(head_dim) ** -0.5

    xp = x_prompt.reshape(bp * sp, d)
    xs = x_sample.reshape(bs * ts, d)
    streams = ((bp, sp), (bs, ts))
    outs = {name: [] for name in ("k_p", "v_p", "lf_p", "pool_p", "conv_p",
                                  "k_s", "v_s", "lf_s", "pool_s", "conv_s")}

    for i in range(depth):
        j = i // 2
        if i % 2 == 0:
            w_in = w_in_cp[j].astype(BF16)
            w_out = w_out_cp[j].astype(BF16)
            new_x = []
            for x, (b, t), tag in zip((xp, xs), streams, ("p", "s")):
                u = norm_matmul_plain(x, norm_mix[i], w_in, BF16).reshape(b, t, 3 * width)
                if tag == "p":
                    hist_p = jnp.zeros((b, pool_hist, width), F32)
                    hist_c = jnp.zeros((b, conv_hist, width), F32)
                    start = 0
                else:
                    hist_p, hist_c, start = state_pool[j], state_conv[j], past
                mix, n_pool, n_conv = cp_mixer(u, hist_p, hist_c, pool_w[j], pool_scale[j], conv_w[j],
                                               conv_b[j], conv_ln_g[j], conv_ln_b[j], start)
                outs["pool_" + tag].append(n_pool)
                outs["conv_" + tag].append(n_conv)
                new_x.append(matmul_residual(mix.reshape(b * t, 2 * width), w_out, x))
            xp, xs = new_x
        else:
            w_in = w_in_fox[j]
            w_q = w_in[:, :att].astype(BF16)
            w_k = w_in[:, att:2 * att].astype(BF16)
            w_v = w_in[:, 2 * att:3 * att].astype(BF16)
            w_f = _pad_cols(w_in[:, 3 * att:], LANES).astype(BF16)
            b_f = jnp.pad(b_forget[j], (0, LANES - n_heads))
            w_out = w_out_fox[j].astype(BF16)
            new_x = []
            for x, (b, t), tag in zip((xp, xs), streams, ("p", "s")):
                g = norm_mix[i]
                (q,) = norm_matmul_heads(x, g, w_q, q_norm[j], head_dim=head_dim, scale=scale, normed=True,
                                         out_dtypes=(BF16,))
                k32, k16 = norm_matmul_heads(x, g, w_k, k_norm[j], head_dim=head_dim, scale=1.0, normed=True,
                                             out_dtypes=(F32, BF16))
                v32, v16 = norm_matmul_heads(x, g, w_v, k_norm[j], head_dim=head_dim, scale=1.0, normed=False,
                                             out_dtypes=(F32, BF16))
                logf = norm_matmul_gate(x, g, w_f, b_f)[:, :n_heads].reshape(b, t, n_heads)
                outs["k_" + tag].append(k32.reshape(b, t, n_heads, head_dim))
                outs["v_" + tag].append(v32.reshape(b, t, n_heads, head_dim))
                outs["lf_" + tag].append(logf)
                q, k16, v16 = (a.reshape(b, t, att) for a in (q, k16, v16))
                if tag == "p":
                    f_row = jnp.cumsum(logf, axis=1).transpose(0, 2, 1).reshape(b, n_heads, 1, t)
                    o = flash_prompt(q, k16, v16, f_row, n_heads=n_heads, head_dim=head_dim)
                else:
                    f_all = jnp.cumsum(jnp.concatenate([cache_logf[j], logf], axis=1), axis=1).transpose(0, 2, 1)
                    o = sample_attention(q, k16, v16, cache_k[j].reshape(b, past, att),
                                         cache_v[j].reshape(b, past, att), f_all[:, :, :past], f_all[:, :, past:],
                                         n_heads=n_heads, head_dim=head_dim)
                new_x.append(matmul_residual(o.reshape(b * t, att), w_out, x))
            xp, xs = new_x

        w_router = _pad_cols(jnp.concatenate([router_group_w[i], router_expert_w[i]], axis=1), LANES)
        b_router = jnp.pad(jnp.concatenate([router_group_b[i], router_expert_b[i]]),
                           (0, LANES - n_groups - n_experts))
        xp, xs = hier_moe([xp, xs], norm_ffn[i], w_router, b_router, expert_w_gate[i].astype(BF16),
                          expert_w_up[i].astype(BF16), expert_w_down[i].astype(BF16),
                          n_groups=n_groups, per_group=per_group)

        w_g = w_ple_gate[i].astype(BF16)
        w_p = w_ple_proj[i].astype(BF16)
        xp = norm_matmul_ple(xp, norm_ple[i], w_g, p_prompt[i].reshape(bp * sp, -1), w_p)
        xs = norm_matmul_ple(xs, norm_ple[i], w_g, p_sample[i].reshape(bs * ts, -1), w_p)

    st = lambda name: jnp.stack(outs[name])
    return (xp.reshape(bp, sp, d), xs.reshape(bs, ts, d), st("k_p"), st("v_p"), st("lf_p"), st("pool_p"),
            st("conv_p"), st("k_s"), st("v_s"), st("lf_s"), st("pool_s"), st("conv_s"))
```

```python
import functools
import math

import jax
import jax.numpy as jnp
from jax import lax
from jax.experimental import pallas as pl
from jax.experimental.pallas import tpu as pltpu

F32 = jnp.float32
BF16 = jnp.bfloat16

NORM_EPS = 1e-6
NEG_INF = -1e30
LOG2E = math.log2(math.e)
POOL_WINDOWS = (2, 4, 8, 16)
TOP_K = 2

V7X_VMEM_BYTES = 64 * 1024 * 1024
VMEM_LIMIT = V7X_VMEM_BYTES - 8 * 1024 * 1024
LANES = 128

ROW_TILE = 512
COL_CHUNK = 512
_NT = (((1,), (1,)), ((), ()))


def _cparams(*sem):
    return pltpu.CompilerParams(dimension_semantics=sem, vmem_limit_bytes=VMEM_LIMIT)


def _pick(n, prefs):
    for p in prefs:
        if n % p == 0:
            return p
    return n


def _rms_rows(x, g):
    ms = jnp.mean(x * x, axis=-1, keepdims=True)
    return x * lax.rsqrt(ms + NORM_EPS) * g


def _resident(shape):
    zeros = (0,) * len(shape)
    return pl.BlockSpec(shape, lambda i: zeros)


def _rows(tm, width):
    return pl.BlockSpec((tm, width), lambda i: (i, 0))


def _split_rows(tm, width, n_first):
    return (pl.BlockSpec((tm, width), lambda i: (jnp.minimum(i, n_first - 1), 0)),
            pl.BlockSpec((tm, width), lambda i: (jnp.maximum(i - n_first, 0), 0)))


def _row_tile(m_first, m_second):
    return _pick(math.gcd(m_first, m_second), (ROW_TILE, 256, 128, 64, 32, 16))


def _cp_in_kernel(x_ref, g_ref, w_ref, o_ref, h_ref):
    h_ref[...] = _rms_rows(x_ref[...], g_ref[...]).astype(BF16)
    for c0 in range(0, o_ref.shape[1], COL_CHUNK):
        o_ref[:, c0:c0 + COL_CHUNK] = jnp.dot(
            h_ref[...], w_ref[:, c0:c0 + COL_CHUNK], preferred_element_type=F32).astype(o_ref.dtype)


def cp_in_proj(x, g, w, tm):
    m, d = x.shape
    n = w.shape[1]
    return pl.pallas_call(
        _cp_in_kernel, name="cp_in_proj",
        grid=(m // tm,),
        in_specs=[_rows(tm, d), _resident((1, d)), _resident((d, n))],
        out_specs=_rows(tm, n),
        out_shape=jax.ShapeDtypeStruct((m, n), BF16),
        scratch_shapes=[pltpu.VMEM((tm, d), BF16)],
        compiler_params=_cparams("parallel"),
    )(x, g.reshape(1, d), w)


def _q_proj_kernel(x_ref, g_ref, w_ref, hg_ref, o_ref, h_ref, *, head_dim, scale):
    h_ref[...] = _rms_rows(x_ref[...], g_ref[...]).astype(BF16)
    for c0 in range(0, o_ref.shape[1], COL_CHUNK):
        acc = jnp.dot(h_ref[...], w_ref[:, c0:c0 + COL_CHUNK], preferred_element_type=F32)
        for h0 in range(0, COL_CHUNK, head_dim):
            a = _rms_rows(acc[:, h0:h0 + head_dim], hg_ref[...]) * scale
            o_ref[:, c0 + h0:c0 + h0 + head_dim] = a.astype(BF16)


def q_proj(x, g, w, head_gain, scale, tm):
    m, d = x.shape
    n = w.shape[1]
    hd = head_gain.shape[0]
    return pl.pallas_call(
        functools.partial(_q_proj_kernel, head_dim=hd, scale=scale), name="q_proj",
        grid=(m // tm,),
        in_specs=[_rows(tm, d), _resident((1, d)), _resident((d, n)), _resident((1, hd))],
        out_specs=_rows(tm, n),
        out_shape=jax.ShapeDtypeStruct((m, n), BF16),
        scratch_shapes=[pltpu.VMEM((tm, d), BF16)],
        compiler_params=_cparams("parallel"),
    )(x, g.reshape(1, d), w, head_gain.reshape(1, hd))


def _kv_proj_kernel(x_ref, g_ref, w_ref, hg_ref, bf_ref, *rest, n_heads, head_dim, n_first, aliased):
    if aliased:
        rest = rest[6:]
    k_ref, v_ref, k4p_ref, v4p_ref, lfp_ref, k4s_ref, v4s_ref, lfs_ref, h_ref = rest
    att = n_heads * head_dim
    h_ref[...] = _rms_rows(x_ref[...], g_ref[...]).astype(BF16)
    i = pl.program_id(0)

    def emit(k4_ref, v4_ref, lf_ref):
        for c0 in range(0, 2 * att, COL_CHUNK):
            acc = jnp.dot(h_ref[...], w_ref[:, c0:c0 + COL_CHUNK], preferred_element_type=F32)
            for h0 in range(0, COL_CHUNK, head_dim):
                col = c0 + h0
                head = (col % att) // head_dim
                a = acc[:, h0:h0 + head_dim]
                if col < att:
                    a = _rms_rows(a, hg_ref[...])
                    k_ref[:, col:col + head_dim] = a.astype(BF16)
                    k4_ref[0, :, head, :] = a
                else:
                    v_ref[:, col - att:col - att + head_dim] = a.astype(BF16)
                    v4_ref[0, :, head, :] = a
        z = jnp.dot(h_ref[...], w_ref[:, 2 * att:2 * att + LANES], preferred_element_type=F32) + bf_ref[...]
        logf = jnp.minimum(z, 0.0) - jnp.log1p(jnp.exp(-jnp.abs(z)))
        lf_ref[0] = logf[:, 0:n_heads]

    @pl.when(i < n_first)
    def _():
        emit(k4p_ref, v4p_ref, lfp_ref)

    @pl.when(i >= n_first)
    def _():
        emit(k4s_ref, v4s_ref, lfs_ref)


def kv_proj(x, g, w, head_gain, b_f, layer, n_layers, m_first, prev, tm):
    m, d = x.shape
    n = w.shape[1]
    hd = head_gain.shape[0]
    n_heads = b_f.shape[0]
    att = n_heads * hd
    n_first = m_first // tm
    m_second = m - m_first
    first = lambda i: (layer, jnp.minimum(i, n_first - 1), 0, 0)
    second = lambda i: (layer, jnp.maximum(i - n_first, 0), 0, 0)
    first3 = lambda i: (layer, jnp.minimum(i, n_first - 1), 0)
    second3 = lambda i: (layer, jnp.maximum(i - n_first, 0), 0)
    kv4 = lambda f: pl.BlockSpec((1, tm, n_heads, hd), f)
    lf3 = lambda f: pl.BlockSpec((1, tm, n_heads), f)
    stacked = lambda rows: jax.ShapeDtypeStruct((n_layers, rows, n_heads, hd), F32)
    out_shape = [jax.ShapeDtypeStruct((m, att), BF16), jax.ShapeDtypeStruct((m, att), BF16),
                 stacked(m_first), stacked(m_first), jax.ShapeDtypeStruct((n_layers, m_first, n_heads), F32),
                 stacked(m_second), stacked(m_second), jax.ShapeDtypeStruct((n_layers, m_second, n_heads), F32)]
    out_specs = [_rows(tm, att), _rows(tm, att), kv4(first), kv4(first), lf3(first3),
                 kv4(second), kv4(second), lf3(second3)]
    in_specs = [_rows(tm, d), _resident((1, d)), _resident((d, n)), _resident((1, hd)), _resident((1, LANES))]
    args = [x, g.reshape(1, d), w, head_gain.reshape(1, hd), jnp.pad(b_f, (0, LANES - n_heads)).reshape(1, LANES)]
    aliases = {}
    if prev is not None:
        in_specs += [pl.BlockSpec(memory_space=pl.ANY)] * 6
        args += list(prev)
        aliases = {5 + t: 2 + t for t in range(6)}
    kern = functools.partial(_kv_proj_kernel, n_heads=n_heads, head_dim=hd, n_first=n_first,
                             aliased=prev is not None)
    return pl.pallas_call(
        kern, name="kv_proj",
        grid=(m // tm,),
        in_specs=in_specs,
        out_specs=out_specs,
        out_shape=out_shape,
        scratch_shapes=[pltpu.VMEM((tm, d), BF16)],
        input_output_aliases=aliases,
        compiler_params=_cparams("arbitrary"),
    )(*args)


def _ple_kernel(x_ref, g_ref, w_ref, pa_ref, pb_ref, wp_ref, o_ref, h_ref, p_ref, *, n_first):
    h_ref[...] = _rms_rows(x_ref[...], g_ref[...]).astype(BF16)
    p_ref[...] = jnp.where(pl.program_id(0) < n_first, pa_ref[...], pb_ref[...]).astype(BF16)
    for c0 in range(0, o_ref.shape[1], COL_CHUNK):
        cs = slice(c0, c0 + COL_CHUNK)
        gate = jax.nn.sigmoid(jnp.dot(h_ref[...], w_ref[:, cs], preferred_element_type=F32))
        proj = jnp.dot(p_ref[...], wp_ref[:, cs], preferred_element_type=F32)
        o_ref[:, cs] = x_ref[:, cs] + gate * proj


def ple_update(x, g, w, p_first, p_second, wp, tm):
    m, d = x.shape
    pd = wp.shape[0]
    n_first = p_first.shape[0] // tm
    return pl.pallas_call(
        functools.partial(_ple_kernel, n_first=n_first), name="ple_update",
        grid=(m // tm,),
        in_specs=[_rows(tm, d), _resident((1, d)), _resident((d, d)), *_split_rows(tm, pd, n_first),
                  _resident((pd, d))],
        out_specs=_rows(tm, d),
        out_shape=jax.ShapeDtypeStruct((m, d), F32),
        scratch_shapes=[pltpu.VMEM((tm, d), BF16), pltpu.VMEM((tm, pd), BF16)],
        compiler_params=_cparams("parallel"),
    )(x, g.reshape(1, d), w, p_first, p_second, wp)


def _out_proj_kernel(aa_ref, ab_ref, w_ref, x_ref, o_ref, a_ref, *, n_first):
    a_ref[...] = jnp.where(pl.program_id(0) < n_first, aa_ref[...], ab_ref[...])
    for c0 in range(0, o_ref.shape[1], COL_CHUNK):
        cs = slice(c0, c0 + COL_CHUNK)
        o_ref[:, cs] = x_ref[:, cs] + jnp.dot(a_ref[...], w_ref[:, cs], preferred_element_type=F32)


def out_proj_residual(a_first, a_second, w, x, tm):
    m, d = x.shape
    k = w.shape[0]
    n_first = a_first.shape[0] // tm
    return pl.pallas_call(
        functools.partial(_out_proj_kernel, n_first=n_first), name="out_proj_residual",
        grid=(m // tm,),
        in_specs=[*_split_rows(tm, k, n_first), _resident((k, d)), _rows(tm, d)],
        out_specs=_rows(tm, d),
        out_shape=jax.ShapeDtypeStruct((m, d), F32),
        scratch_shapes=[pltpu.VMEM((tm, k), BF16)],
        compiler_params=_cparams("parallel"),
    )(a_first, a_second, w, x)


POOL_HIST_ROWS = 16
CONV_HIST_ROWS = 32


def _cp_mixer_kernel(u_ref, hp_ref, hc_ref, pw_ref, ps_ref, cw_ref, cb_ref, lg_ref, lb_ref,
                     mix_ref, npool_ref, nconv_ref, extp_ref, extc_ref, conv_ref,
                     *, tt, width, start_pos, conv_k, row_chunk, col_chunk):
    ti = pl.program_id(1)
    n_t = pl.num_programs(1)
    ph, ch = POOL_HIST_ROWS, CONV_HIST_ROWS
    w = width
    gd = w // len(POOL_WINDOWS)

    @pl.when(ti == 0)
    def _():
        extp_ref[0:ph, :] = hp_ref[0]
        extc_ref[0:ch, :] = hc_ref[0]

    @pl.when(ti > 0)
    def _():
        extp_ref[0:ph, :] = extp_ref[tt:tt + ph, :]
        extc_ref[0:ch, :] = extc_ref[tt:tt + ch, :]

    extp_ref[ph:ph + tt, :] = u_ref[:, 0:w].astype(F32)
    ua = u_ref[:, w:2 * w].astype(F32)
    ub = u_ref[:, 2 * w:3 * w].astype(F32)
    extc_ref[ch:ch + tt, :] = ua * jax.nn.sigmoid(ub)

    for r0 in range(0, tt, row_chunk):
        pos = (start_pos + 1 + r0 + ti * tt
               + lax.broadcasted_iota(jnp.int32, (row_chunk, 1), 0)).astype(F32)
        for g, win in enumerate(POOL_WINDOWS):
            c0 = g * gd
            cur = extp_ref[ph + r0:ph + r0 + row_chunk, c0:c0 + gd]
            acc = cur
            for j in range(1, win):
                acc = acc + extp_ref[ph + r0 - j:ph + r0 - j + row_chunk, c0:c0 + gd]
            pooled = acc / jnp.minimum(pos, float(win)) - cur
            po = jnp.dot(pooled.astype(BF16), pw_ref[g], preferred_element_type=F32)
            mix_ref[r0:r0 + row_chunk, c0:c0 + gd] = (po * ps_ref[:, c0:c0 + gd]).astype(BF16)

    base = ch - (conv_k - 1)
    for r0 in range(0, tt, row_chunk):
        for c0 in range(0, w, col_chunk):
            acc = jnp.broadcast_to(cb_ref[:, c0:c0 + col_chunk], (row_chunk, col_chunk))
            for k in range(conv_k):
                acc = acc + (cw_ref[k:k + 1, c0:c0 + col_chunk]
                             * extc_ref[base + r0 + k:base + r0 + k + row_chunk, c0:c0 + col_chunk])
            conv_ref[r0:r0 + row_chunk, c0:c0 + col_chunk] = acc

    for r0 in range(0, tt, row_chunk):
        c = conv_ref[r0:r0 + row_chunk, :]
        mu = jnp.mean(c, axis=-1, keepdims=True)
        cc = c - mu
        var = jnp.mean(cc * cc, axis=-1, keepdims=True)
        y = cc * lax.rsqrt(var + NORM_EPS) * lg_ref[...] + lb_ref[...]
        mix_ref[r0:r0 + row_chunk, w:2 * w] = (y * jax.nn.sigmoid(y)).astype(BF16)

    @pl.when(ti == n_t - 1)
    def _():
        npool_ref[0] = extp_ref[ph + tt - (ph - 1):ph + tt, :]
        nconv_ref[0] = extc_ref[ch + tt - (conv_k - 1):ch + tt, :]


def cp_mixer(u, row0, b, t, hist_pool, hist_conv, pool_w, pool_scale, conv_w, conv_b, ln_g, ln_b, start_pos):
    w = u.shape[1] // 3
    conv_k = conv_w.shape[0]
    n_pool_hist = hist_pool.shape[1]
    assert n_pool_hist == POOL_HIST_ROWS - 1 and conv_k - 1 <= CONV_HIST_ROWS
    tt = _pick(t, (256, 128, 64, 32, 16))
    assert row0 % tt == 0
    n_t = t // tt
    blk0 = row0 // tt
    hp = jnp.pad(hist_pool, ((0, 0), (POOL_HIST_ROWS - n_pool_hist, 0), (0, 0)))
    hc = jnp.pad(hist_conv, ((0, 0), (CONV_HIST_ROWS - (conv_k - 1), 0), (0, 0)))
    kern = functools.partial(_cp_mixer_kernel, tt=tt, width=w, start_pos=start_pos, conv_k=conv_k,
                             row_chunk=min(tt, 64), col_chunk=2 * LANES)
    n_g, gd = pool_w.shape[0], pool_w.shape[1]
    vec = lambda: pl.BlockSpec((1, w), lambda i, j: (0, 0))
    return pl.pallas_call(
        kern, name="cp_mixer",
        grid=(b, n_t),
        in_specs=[
            pl.BlockSpec((tt, 3 * w), lambda i, j: (blk0 + i * n_t + j, 0)),
            pl.BlockSpec((1, POOL_HIST_ROWS, w), lambda i, j: (i, 0, 0)),
            pl.BlockSpec((1, CONV_HIST_ROWS, w), lambda i, j: (i, 0, 0)),
            pl.BlockSpec((n_g, gd, gd), lambda i, j: (0, 0, 0)),
            vec(),
            pl.BlockSpec((conv_k, w), lambda i, j: (0, 0)),
            vec(), vec(), vec(),
        ],
        out_specs=[
            pl.BlockSpec((tt, 2 * w), lambda i, j: (i * n_t + j, 0)),
            pl.BlockSpec((1, n_pool_hist, w), lambda i, j: (i, 0, 0)),
            pl.BlockSpec((1, conv_k - 1, w), lambda i, j: (i, 0, 0)),
        ],
        out_shape=[
            jax.ShapeDtypeStruct((b * t, 2 * w), BF16),
            jax.ShapeDtypeStruct((b, n_pool_hist, w), F32),
            jax.ShapeDtypeStruct((b, conv_k - 1, w), F32),
        ],
        scratch_shapes=[
            pltpu.VMEM((POOL_HIST_ROWS + tt, w), F32),
            pltpu.VMEM((CONV_HIST_ROWS + tt, w), F32),
            pltpu.VMEM((tt, w), F32),
        ],
        compiler_params=_cparams("arbitrary", "arbitrary"),
    )(u, hp, hc, pool_w.astype(BF16), pool_scale.reshape(1, w), conv_w, conv_b.reshape(1, w),
      ln_g.reshape(1, w), ln_b.reshape(1, w))


FLASH_HEADS_PER_STEP = 2
FLASH_BLOCK = 512


def _flash_kernel(q_ref, k_ref, v_ref, f_ref, o_ref, vt_ref, fcol_ref, *, t, n_t, head_dim, n_hp):
    s_len = t * n_t
    hd = head_dim
    for hh in range(n_hp):
        for c in range(0, s_len, t):
            vt_ref[hh, :, c:c + t] = jnp.transpose(v_ref[c:c + t, hh * hd:(hh + 1) * hd].astype(F32)).astype(BF16)
        for c in range(0, s_len, LANES):
            f_chunk = f_ref[0, hh, :, c:c + LANES] * LOG2E
            fcol_ref[hh, c:c + LANES, :] = jnp.transpose(jnp.broadcast_to(f_chunk, (LANES, LANES)))

    below_diag = (lax.broadcasted_iota(jnp.int32, (t, t), 0) <= lax.broadcasted_iota(jnp.int32, (t, t), 1))

    def q_block(i, _):
        q0 = pl.multiple_of(i * t, t)
        qs = [q_ref[pl.ds(q0, t), hh * hd:(hh + 1) * hd] for hh in range(n_hp)]
        fqs = [f_ref[0, hh, :, pl.ds(q0, t)] * LOG2E for hh in range(n_hp)]

        def step(j, carry, masked):
            k0 = pl.multiple_of(j * t, t)
            out = []
            for hh in range(n_hp):
                m, l, acc = carry[hh]
                k = k_ref[pl.ds(k0, t), hh * hd:(hh + 1) * hd]
                f_k = fcol_ref[hh, pl.ds(k0, t), :]
                s = lax.dot_general(k, qs[hh], _NT, preferred_element_type=F32)
                s = s + (fqs[hh] - jnp.tile(f_k, (1, t // LANES)))
                if masked:
                    s = jnp.where(below_diag, s, NEG_INF)
                m_new = jnp.maximum(m, jnp.max(s, axis=0, keepdims=True))
                alpha = jnp.exp2(m - m_new)
                p = jnp.exp2(s - m_new)
                l_new = alpha * l + jnp.sum(p, axis=0, keepdims=True)
                pv = jnp.dot(vt_ref[hh, :, pl.ds(k0, t)], p.astype(BF16), preferred_element_type=F32)
                out.append((m_new, l_new, alpha * acc + pv))
            return tuple(out)

        init = tuple((jnp.full((1, t), NEG_INF, F32), jnp.zeros((1, t), F32), jnp.zeros((hd, t), F32))
                     for _ in range(n_hp))
        carry = lax.fori_loop(0, i, lambda j, c: step(j, c, False), init)
        carry = step(i, carry, True)
        for hh in range(n_hp):
            _, l, acc = carry[hh]
            o_ref[pl.ds(q0, t), hh * hd:(hh + 1) * hd] = jnp.transpose(acc / l).astype(o_ref.dtype)
        return 0

    lax.fori_loop(0, n_t, q_block, 0)


def flash_prompt(q, k, v, f_row, b, s_len, *, n_heads, head_dim):
    n_hp = FLASH_HEADS_PER_STEP
    t = _pick(s_len, (FLASH_BLOCK, 256, 128))
    kern = functools.partial(_flash_kernel, t=t, n_t=s_len // t, head_dim=head_dim, n_hp=n_hp)
    heads = lambda: pl.BlockSpec((s_len, n_hp * head_dim), lambda i, h: (i, h))
    return pl.pallas_call(
        kern, name="flash_prompt",
        grid=(b, n_heads // n_hp),
        in_specs=[heads(), heads(), heads(), pl.BlockSpec((1, n_hp, 1, s_len), lambda i, h: (i, h, 0, 0))],
        out_specs=heads(),
        out_shape=jax.ShapeDtypeStruct((b * s_len, n_heads * head_dim), BF16),
        scratch_shapes=[pltpu.VMEM((n_hp, head_dim, s_len), BF16), pltpu.VMEM((n_hp, s_len, LANES), F32)],
        compiler_params=_cparams("parallel", "parallel"),
    )(q, k, v, f_row)


def _sample_attn_kernel(q_ref, kn_ref, vn_ref, kc_ref, vc_ref, fp_ref, fn_ref, o_ref,
                        qbd_ref, m_ref, l_ref, acc_ref, *, t, n_heads, head_dim):
    j = pl.program_id(1)
    n_j = pl.num_programs(1)
    r = n_heads * t
    hd = n_heads * head_dim

    def same_head():
        return (lax.broadcasted_iota(jnp.int32, (r, hd), 0) // t
                == lax.broadcasted_iota(jnp.int32, (r, hd), 1) // head_dim)

    def rep_rows(x):
        return jnp.broadcast_to(x[:, None, :], (n_heads, t, x.shape[-1])).reshape(r, x.shape[-1])

    f_new = rep_rows(fn_ref[0]) * LOG2E
    row_t = lax.broadcasted_iota(jnp.int32, (r, t), 0) % t
    col_t = lax.broadcasted_iota(jnp.int32, (r, t), 1)
    f_q = jnp.sum(jnp.where(col_t == row_t, f_new, 0.0), axis=-1, keepdims=True)

    @pl.when(j == 0)
    def _():
        q_rep = jnp.broadcast_to(q_ref[...][None], (n_heads, t, hd)).reshape(r, hd)
        qbd_ref[...] = jnp.where(same_head(), q_rep, jnp.zeros_like(q_rep))
        m_ref[...] = jnp.full(m_ref.shape, NEG_INF, F32)
        l_ref[...] = jnp.zeros(l_ref.shape, F32)
        acc_ref[...] = jnp.zeros(acc_ref.shape, F32)

    def update(k, v, bias, valid):
        s = lax.dot_general(qbd_ref[...], k, _NT, preferred_element_type=F32) + bias
        if valid is not None:
            s = jnp.where(valid, s, NEG_INF)
        m = m_ref[...]
        m_new = jnp.maximum(m, jnp.max(s, axis=-1, keepdims=True))
        alpha = jnp.exp2(m - m_new)
        p = jnp.exp2(s - m_new)
        l_ref[...] = alpha * l_ref[...] + jnp.sum(p, axis=-1, keepdims=True)
        acc_ref[...] = alpha * acc_ref[...] + jnp.dot(p.astype(BF16), v, preferred_element_type=F32)
        m_ref[...] = m_new

    update(kc_ref[0].astype(BF16), vc_ref[0].astype(BF16), f_q - rep_rows(fp_ref[0]) * LOG2E, None)

    @pl.when(j == n_j - 1)
    def _():
        update(kn_ref[...], vn_ref[...], f_q - f_new, col_t <= row_t)
        o_full = jnp.where(same_head(), acc_ref[...] / l_ref[...], 0.0)
        o = o_full[0:t]
        for h in range(1, n_heads):
            o = o + o_full[h * t:(h + 1) * t]
        o_ref[...] = o.astype(o_ref.dtype)


def sample_attention(q, k, v, row0, b, t, cache_k, cache_v, f_past, f_new, *, n_heads, head_dim):
    hd = n_heads * head_dim
    p = cache_k.shape[1]
    tp = _pick(p, (512, 256, 128))
    r = n_heads * t
    assert row0 % t == 0
    blk0 = row0 // t
    kern = functools.partial(_sample_attn_kernel, t=t, n_heads=n_heads, head_dim=head_dim)
    new = lambda: pl.BlockSpec((t, hd), lambda i, j: (blk0 + i, 0))
    past = lambda: pl.BlockSpec((1, tp, hd), lambda i, j: (i, j, 0))
    return pl.pallas_call(
        kern, name="sample_attention",
        grid=(b, p // tp),
        in_specs=[new(), new(), new(), past(), past(),
                  pl.BlockSpec((1, n_heads, tp), lambda i, j: (i, 0, j)),
                  pl.BlockSpec((1, n_heads, t), lambda i, j: (i, 0, 0))],
        out_specs=pl.BlockSpec((t, hd), lambda i, j: (i, 0)),
        out_shape=jax.ShapeDtypeStruct((b * t, hd), BF16),
        scratch_shapes=[pltpu.VMEM((r, hd), BF16), pltpu.VMEM((r, 1), F32), pltpu.VMEM((r, 1), F32),
                        pltpu.VMEM((r, hd), F32)],
        compiler_params=_cparams("parallel", "arbitrary"),
    )(q, k, v, cache_k, cache_v, f_past, f_new)


def _router_kernel(x_ref, g_ref, w_ref, b_ref, ids_ref, wts_ref, *, n_groups, per_group):
    h = _rms_rows(x_ref[...], g_ref[...])
    logits = jnp.dot(h, w_ref[...], preferred_element_type=F32,
                     precision=lax.Precision.HIGHEST) + b_ref[...]
    tm, n = logits.shape
    col = lax.broadcasted_iota(jnp.int32, (tm, n), 1)
    neg = jnp.float32(-jnp.inf)

    def first_max(vals):
        top = jnp.max(vals, axis=-1, keepdims=True)
        idx = jnp.min(jnp.where(vals == top, col, n), axis=-1, keepdims=True)
        return top, idx

    gl = jnp.where(col < n_groups, logits, neg)
    g_top, g_sel = first_max(gl)
    g_prob = 1.0 / jnp.sum(jnp.exp(gl - g_top), axis=-1, keepdims=True)
    lo = n_groups + g_sel * per_group
    el = jnp.where((col >= lo) & (col < lo + per_group), logits, neg)
    v1, i1 = first_max(el)
    v2, i2 = first_max(jnp.where(col == i1, neg, el))
    e2 = jnp.exp(v2 - v1)
    w1 = g_prob / (1.0 + e2)
    w2 = g_prob * e2 / (1.0 + e2)
    ids_ref[...] = jnp.where(col == 0, i1 - n_groups, jnp.where(col == 1, i2 - n_groups, 0))
    wts_ref[...] = jnp.where(col == 0, w1, jnp.where(col == 1, w2, 0.0))


def router(x, g, w_pad, b_pad, *, n_groups, per_group):
    m, d = x.shape
    n = w_pad.shape[1]
    tm = _pick(m, (256, 128, 64, 32, 16, 8))
    kern = functools.partial(_router_kernel, n_groups=n_groups, per_group=per_group)
    return pl.pallas_call(
        kern, name="router",
        grid=(m // tm,),
        in_specs=[_rows(tm, d), _resident((1, d)), _resident((d, n)), _resident((1, n))],
        out_specs=[_rows(tm, n), _rows(tm, n)],
        out_shape=[jax.ShapeDtypeStruct((m, n), jnp.int32), jax.ShapeDtypeStruct((m, n), F32)],
        compiler_params=_cparams("parallel"),
    )(x, g.reshape(1, d), w_pad, b_pad.reshape(1, n))


EXPERT_BLOCK_ROWS = 256


def _row_copy(src_ref, src_row, dst_ref, dst_row, sem):
    return pltpu.make_async_copy(src_ref.at[pl.ds(src_row, 1)], dst_ref.at[pl.ds(dst_row, 1)], sem)


def _dispatch_kernel(dest_ref, x_ref, buf_in_ref, buf_ref, sem, *, tm):
    del buf_in_ref

    def start(r, _):
        for k in range(TOP_K):
            _row_copy(x_ref, r, buf_ref, dest_ref[0, 0, TOP_K * r + k], sem).start()
        return 0

    def wait(r, _):
        for k in range(TOP_K):
            _row_copy(x_ref, r, buf_ref, dest_ref[0, 0, TOP_K * r + k], sem).wait()
        return 0

    lax.fori_loop(0, tm, start, 0)
    lax.fori_loop(0, tm, wait, 0)


def dispatch_rows(x, dest, buf):
    m, d = x.shape
    tm = _pick(m, (256, 128, 64, 32, 16, 8))
    dest3 = dest.reshape(m // tm, 1, tm * TOP_K)
    return pl.pallas_call(
        functools.partial(_dispatch_kernel, tm=tm), name="moe_dispatch",
        grid=(m // tm,),
        in_specs=[pl.BlockSpec((1, 1, tm * TOP_K), lambda i: (i, 0, 0), memory_space=pltpu.SMEM),
                  _rows(tm, d),
                  pl.BlockSpec(memory_space=pl.ANY)],
        out_specs=pl.BlockSpec(memory_space=pl.ANY),
        out_shape=jax.ShapeDtypeStruct(buf.shape, buf.dtype),
        scratch_shapes=[pltpu.SemaphoreType.DMA(())],
        input_output_aliases={2: 0},
        compiler_params=_cparams("arbitrary"),
    )(dest3, x, buf)


def _expert_kernel(be_ref, nu_ref, x_ref, g_ref, wg_ref, wu_ref, wd_ref, o_ref, wg_b, wu_b, wd_b):
    i = pl.program_id(0)
    used = i < nu_ref[0]

    @pl.when(used & ((i == 0) | (be_ref[i] != be_ref[jnp.maximum(i - 1, 0)])))
    def _():
        wg_b[...] = wg_ref[0].astype(BF16)
        wu_b[...] = wu_ref[0].astype(BF16)
        wd_b[...] = wd_ref[0].astype(BF16)

    @pl.when(used)
    def _():
        h = _rms_rows(x_ref[...], g_ref[...]).astype(BF16)
        gate = jnp.dot(h, wg_b[...], preferred_element_type=F32)
        up = jnp.dot(h, wu_b[...], preferred_element_type=F32)
        act = (gate * jax.nn.sigmoid(gate) * up).astype(BF16)
        o_ref[...] = jnp.dot(act, wd_b[...], preferred_element_type=F32)

    @pl.when(jnp.logical_not(used))
    def _():
        o_ref[...] = jnp.zeros(o_ref.shape, o_ref.dtype)


def expert_mlps(buf, g, w_gate, w_up, w_down, block_expert, n_used):
    rows, d = buf.shape
    n_e, _, de = w_gate.shape
    br = EXPERT_BLOCK_ROWS
    nb = rows // br
    blk = lambda i, be, nu: (jnp.minimum(i, nu[0] - 1), 0)
    wsel = lambda i, be, nu: (be[i], 0, 0)
    return pl.pallas_call(
        _expert_kernel, name="expert_mlps",
        grid_spec=pltpu.PrefetchScalarGridSpec(
            num_scalar_prefetch=2,
            grid=(nb,),
            in_specs=[pl.BlockSpec((br, d), blk),
                      pl.BlockSpec((1, d), lambda i, be, nu: (0, 0)),
                      pl.BlockSpec((1, d, de), wsel),
                      pl.BlockSpec((1, d, de), wsel),
                      pl.BlockSpec((1, de, d), wsel)],
            out_specs=pl.BlockSpec((br, d), lambda i, be, nu: (i, 0)),
            scratch_shapes=[pltpu.VMEM((d, de), BF16), pltpu.VMEM((d, de), BF16), pltpu.VMEM((de, d), BF16)],
        ),
        out_shape=jax.ShapeDtypeStruct((rows, d), F32),
        compiler_params=_cparams("arbitrary"),
    )(block_expert, n_used, buf, g.reshape(1, d), w_gate, w_up, w_down)


def _combine_kernel(dest_cur_ref, dest_nxt_ref, x_ref, wts_ref, y_ref, o_ref, rows_ref, sems, *, tm):
    i = pl.program_id(0)
    n = pl.num_programs(0)

    def copies(dest_ref, slot, r):
        return [_row_copy(y_ref, dest_ref[0, 0, TOP_K * r + k], rows_ref.at[slot, k], r, sems.at[slot])
                for k in range(TOP_K)]

    def start_all(dest_ref, slot):
        def body(r, _):
            for c in copies(dest_ref, slot, r):
                c.start()
            return 0
        lax.fori_loop(0, tm, body, 0)

    @pl.when(i == 0)
    def _():
        start_all(dest_cur_ref, 0)

    @pl.when(i + 1 < n)
    def _():
        start_all(dest_nxt_ref, (i + 1) % 2)

    slot = i % 2

    def wait_body(r, _):
        for c in copies(dest_cur_ref, slot, r):
            c.wait()
        return 0
    lax.fori_loop(0, tm, wait_body, 0)

    wts = wts_ref[...]
    o_ref[...] = (x_ref[...] + wts[:, 0:1] * rows_ref[slot, 0] + wts[:, 1:2] * rows_ref[slot, 1])


def combine_rows(x, wts, y, dest):
    m, d = x.shape
    tm = _pick(m, (256, 128, 64, 32, 16, 8))
    n_t = m // tm
    dest3 = dest.reshape(n_t, 1, tm * TOP_K)
    smem = lambda f: pl.BlockSpec((1, 1, tm * TOP_K), f, memory_space=pltpu.SMEM)
    return pl.pallas_call(
        functools.partial(_combine_kernel, tm=tm), name="moe_combine",
        grid=(n_t,),
        in_specs=[smem(lambda i: (i, 0, 0)),
                  smem(lambda i: (jnp.minimum(i + 1, n_t - 1), 0, 0)),
                  _rows(tm, d),
                  _rows(tm, wts.shape[1]),
                  pl.BlockSpec(memory_space=pl.ANY)],
        out_specs=_rows(tm, d),
        out_shape=jax.ShapeDtypeStruct((m, d), F32),
        scratch_shapes=[pltpu.VMEM((2, TOP_K, tm, d), F32), pltpu.SemaphoreType.DMA((2,))],
        compiler_params=_cparams("arbitrary"),
    )(dest3, dest3, x, wts, y)


def _routing_tables(flat_e, n_experts, block_rows):
    a = flat_e.shape[0]
    onehot = (flat_e[:, None] == jnp.arange(n_experts, dtype=jnp.int32)[None, :]).astype(jnp.int32)
    csum = jnp.cumsum(onehot, axis=0)
    rank = jnp.sum(csum * onehot, axis=1) - 1
    counts = csum[-1]
    padded = (counts + block_rows - 1) // block_rows * block_rows
    pad_end = jnp.cumsum(padded)
    pad_start = pad_end - padded
    dest = pad_start[flat_e] + rank
    n_blocks = -(-a // block_rows) + n_experts
    block_expert = jnp.minimum(
        jnp.searchsorted(pad_end, jnp.arange(n_blocks, dtype=jnp.int32) * block_rows, side='right'),
        n_experts - 1).astype(jnp.int32)
    n_used = (pad_end[-1:] // block_rows).astype(jnp.int32)
    return dest.astype(jnp.int32), block_expert, n_used, n_blocks


def hier_moe(x, g, w_router, b_router, w_gate, w_up, w_down, *, n_groups, per_group):
    n_experts = n_groups * per_group
    ids, wts = router(x, g, w_router, b_router, n_groups=n_groups, per_group=per_group)
    dest, block_expert, n_used, n_blocks = _routing_tables(ids[:, :TOP_K].reshape(-1), n_experts,
                                                           EXPERT_BLOCK_ROWS)
    buf = dispatch_rows(x, dest, jnp.zeros((n_blocks * EXPERT_BLOCK_ROWS, x.shape[1]), F32))
    y = expert_mlps(buf, g, w_gate, w_up, w_down, block_expert, n_used)
    return combine_rows(x, wts, y, dest)


def _pad_cols(w, n):
    return jnp.pad(w, ((0, 0), (0, n - w.shape[1])))


def kernel(x_prompt, x_sample, cache_k, cache_v, cache_logf, state_pool, state_conv, p_prompt, p_sample,
           norm_mix, norm_ffn, norm_ple, w_in_cp, pool_w, pool_scale, conv_w, conv_b, conv_ln_g, conv_ln_b,
           w_out_cp, w_in_fox, b_forget, q_norm, k_norm, w_out_fox, router_group_w, router_group_b,
           router_expert_w, router_expert_b, expert_w_gate, expert_w_up, expert_w_down, w_ple_gate, w_ple_proj):
    bp, sp, d = x_prompt.shape
    bs, ts, _ = x_sample.shape
    depth = norm_mix.shape[0]
    n_fox = w_in_fox.shape[0]
    n_heads = b_forget.shape[1]
    head_dim = q_norm.shape[1]
    att = n_heads * head_dim
    past = cache_k.shape[2]
    n_groups = router_group_w.shape[2]
    n_experts = router_expert_w.shape[2]
    per_group = n_experts // n_groups
    width = pool_w.shape[1] * pool_w.shape[2]
    q_scale = float(head_dim) ** -0.5 * LOG2E

    mp, ms = bp * sp, bs * ts
    tm = _row_tile(mp, ms)
    x = jnp.concatenate([x_prompt.reshape(mp, d), x_sample.reshape(ms, d)], axis=0)
    pools_p, convs_p, pools_s, convs_s = [], [], [], []
    fox_out = None

    for i in range(depth):
        j = i // 2
        if i % 2 == 0:
            u = cp_in_proj(x, norm_mix[i], w_in_cp[j].astype(BF16), tm)
            mixer = functools.partial(cp_mixer, pool_w=pool_w[j], pool_scale=pool_scale[j], conv_w=conv_w[j],
                                      conv_b=conv_b[j], ln_g=conv_ln_g[j], ln_b=conv_ln_b[j])
            mix_p, n_pool, n_conv = mixer(u, 0, bp, sp, jnp.zeros((bp,) + state_pool.shape[2:], F32),
                                          jnp.zeros((bp,) + state_conv.shape[2:], F32), start_pos=0)
            pools_p.append(n_pool)
            convs_p.append(n_conv)
            mix_s, n_pool, n_conv = mixer(u, mp, bs, ts, state_pool[j], state_conv[j], start_pos=past)
            pools_s.append(n_pool)
            convs_s.append(n_conv)
            x = out_proj_residual(mix_p, mix_s, w_out_cp[j].astype(BF16), x, tm)
        else:
            w_in = w_in_fox[j]
            g = norm_mix[i]
            q = q_proj(x, g, w_in[:, :att].astype(BF16), q_norm[j], q_scale, tm)
            w_kvf = _pad_cols(w_in[:, att:], 2 * att + LANES).astype(BF16)
            res = kv_proj(x, g, w_kvf, k_norm[j], b_forget[j], j, n_fox, mp, fox_out, min(tm, 256))
            k16, v16 = res[0], res[1]
            fox_out = res[2:]
            lf_p = fox_out[2][j].reshape(bp, sp, n_heads)
            lf_s = fox_out[5][j].reshape(bs, ts, n_heads)
            f_row = jnp.cumsum(lf_p, axis=1).transpose(0, 2, 1).reshape(bp, n_heads, 1, sp)
            o_p = flash_prompt(q, k16, v16, f_row, bp, sp, n_heads=n_heads, head_dim=head_dim)
            f_all = jnp.cumsum(jnp.concatenate([cache_logf[j], lf_s], axis=1), axis=1).transpose(0, 2, 1)
            o_s = sample_attention(q, k16, v16, mp, bs, ts, cache_k[j].reshape(bs, past, att),
                                   cache_v[j].reshape(bs, past, att), f_all[:, :, :past], f_all[:, :, past:],
                                   n_heads=n_heads, head_dim=head_dim)
            x = out_proj_residual(o_p, o_s, w_out_fox[j].astype(BF16), x, tm)

        w_router = _pad_cols(jnp.concatenate([router_group_w[i], router_expert_w[i]], axis=1), LANES)
        b_router = jnp.pad(jnp.concatenate([router_group_b[i], router_expert_b[i]]),
                           (0, LANES - n_groups - n_experts))
        x = hier_moe(x, norm_ffn[i], w_router, b_router, expert_w_gate[i], expert_w_up[i], expert_w_down[i],
                     n_groups=n_groups, per_group=per_group)
        x = ple_update(x, norm_ple[i], w_ple_gate[i].astype(BF16), p_prompt[i].reshape(mp, -1),
                       p_sample[i].reshape(ms, -1), w_ple_proj[i].astype(BF16), tm)

    k4p, v4p, lfp, k4s, v4s, lfs = fox_out
    return (x[:mp].reshape(bp, sp, d), x[mp:].reshape(bs, ts, d),
            k4p.reshape(n_fox, bp, sp, n_heads, head_dim), v4p.reshape(n_fox, bp, sp, n_heads, head_dim),
            lfp.reshape(n_fox, bp, sp, n_heads), jnp.stack(pools_p), jnp.stack(convs_p),
            k4s.reshape(n_fox, bs, ts, n_heads, head_dim), v4s.reshape(n_fox, bs, ts, n_heads, head_dim),
            lfs.reshape(n_fox, bs, ts, n_heads), jnp.stack(pools_s), jnp.stack(convs_s))
```

```python
import functools
import math

import jax
import jax.numpy as jnp
from jax import lax
from jax.experimental import pallas as pl
from jax.experimental.pallas import tpu as pltpu

F32 = jnp.float32
BF16 = jnp.bfloat16

NORM_EPS = 1e-6
NEG_INF = -1e30
LOG2E = math.log2(math.e)
POOL_WINDOWS = (2, 4, 8, 16)
TOP_K = 2

V7X_VMEM_BYTES = 64 * 1024 * 1024
VMEM_LIMIT = V7X_VMEM_BYTES - 8 * 1024 * 1024
LANES = 128
SUBLANES = 8

ROW_TILE = 512
COL_CHUNK = 512
_NT = (((1,), (1,)), ((), ()))


def _cparams(*sem):
    return pltpu.CompilerParams(dimension_semantics=sem, vmem_limit_bytes=VMEM_LIMIT)


def _pick(n, prefs):
    for p in prefs:
        if n % p == 0:
            return p
    return n


def _rms_rows(x, g):
    ms = jnp.mean(x * x, axis=-1, keepdims=True)
    return x * lax.rsqrt(ms + NORM_EPS) * g


def _resident(shape):
    zeros = (0,) * len(shape)
    return pl.BlockSpec(shape, lambda i: zeros)


def _rows(tm, width):
    return pl.BlockSpec((tm, width), lambda i: (i, 0))


def _split_rows(tm, width, n_first):
    return (pl.BlockSpec((tm, width), lambda i: (jnp.minimum(i, n_first - 1), 0)),
            pl.BlockSpec((tm, width), lambda i: (jnp.maximum(i - n_first, 0), 0)))


def _row_tile(m_first, m_second):
    return _pick(math.gcd(m_first, m_second), (ROW_TILE, 256, 128, 64, 32, 16))


def _cp_in_kernel(x_ref, g_ref, w_ref, o_ref, h_ref):
    h_ref[...] = _rms_rows(x_ref[...], g_ref[...]).astype(BF16)
    for c0 in range(0, o_ref.shape[1], COL_CHUNK):
        o_ref[:, c0:c0 + COL_CHUNK] = jnp.dot(
            h_ref[...], w_ref[:, c0:c0 + COL_CHUNK], preferred_element_type=F32).astype(o_ref.dtype)


def cp_in_proj(x, g, w, tm):
    m, d = x.shape
    n = w.shape[1]
    return pl.pallas_call(
        _cp_in_kernel, name="cp_in_proj",
        grid=(m // tm,),
        in_specs=[_rows(tm, d), _resident((1, d)), _resident((d, n))],
        out_specs=_rows(tm, n),
        out_shape=jax.ShapeDtypeStruct((m, n), BF16),
        scratch_shapes=[pltpu.VMEM((tm, d), BF16)],
        compiler_params=_cparams("parallel"),
    )(x, g.reshape(1, d), w)


def _q_proj_kernel(x_ref, g_ref, w_ref, hg_ref, o_ref, h_ref, *, head_dim, scale):
    h_ref[...] = _rms_rows(x_ref[...], g_ref[...]).astype(BF16)
    for c0 in range(0, o_ref.shape[1], COL_CHUNK):
        acc = jnp.dot(h_ref[...], w_ref[:, c0:c0 + COL_CHUNK], preferred_element_type=F32)
        for h0 in range(0, COL_CHUNK, head_dim):
            a = _rms_rows(acc[:, h0:h0 + head_dim], hg_ref[...]) * scale
            o_ref[:, c0 + h0:c0 + h0 + head_dim] = a.astype(BF16)


def q_proj(x, g, w, head_gain, scale, tm):
    m, d = x.shape
    n = w.shape[1]
    hd = head_gain.shape[0]
    return pl.pallas_call(
        functools.partial(_q_proj_kernel, head_dim=hd, scale=scale), name="q_proj",
        grid=(m // tm,),
        in_specs=[_rows(tm, d), _resident((1, d)), _resident((d, n)), _resident((1, hd))],
        out_specs=_rows(tm, n),
        out_shape=jax.ShapeDtypeStruct((m, n), BF16),
        scratch_shapes=[pltpu.VMEM((tm, d), BF16)],
        compiler_params=_cparams("parallel"),
    )(x, g.reshape(1, d), w, head_gain.reshape(1, hd))


def _kv_proj_kernel(x_ref, g_ref, w_ref, hg_ref, bf_ref, *rest, n_heads, head_dim, n_first, aliased):
    if aliased:
        rest = rest[6:]
    k_ref, v_ref, k4p_ref, v4p_ref, lfp_ref, k4s_ref, v4s_ref, lfs_ref, h_ref = rest
    att = n_heads * head_dim
    h_ref[...] = _rms_rows(x_ref[...], g_ref[...]).astype(BF16)
    i = pl.program_id(0)

    def emit(k4_ref, v4_ref, lf_ref):
        for c0 in range(0, 2 * att, COL_CHUNK):
            acc = jnp.dot(h_ref[...], w_ref[:, c0:c0 + COL_CHUNK], preferred_element_type=F32)
            for h0 in range(0, COL_CHUNK, head_dim):
                col = c0 + h0
                head = (col % att) // head_dim
                a = acc[:, h0:h0 + head_dim]
                if col < att:
                    a = _rms_rows(a, hg_ref[...])
                    k_ref[:, col:col + head_dim] = a.astype(BF16)
                    k4_ref[0, :, head, :] = a
                else:
                    v_ref[:, col - att:col - att + head_dim] = a.astype(BF16)
                    v4_ref[0, :, head, :] = a
        z = jnp.dot(h_ref[...], w_ref[:, 2 * att:2 * att + LANES], preferred_element_type=F32) + bf_ref[...]
        logf = jnp.minimum(z, 0.0) - jnp.log1p(jnp.exp(-jnp.abs(z)))
        lf_ref[0] = logf[:, 0:n_heads]

    @pl.when(i < n_first)
    def _():
        emit(k4p_ref, v4p_ref, lfp_ref)

    @pl.when(i >= n_first)
    def _():
        emit(k4s_ref, v4s_ref, lfs_ref)


def kv_proj(x, g, w, head_gain, b_f, layer, n_layers, m_first, prev, tm):
    m, d = x.shape
    n = w.shape[1]
    hd = head_gain.shape[0]
    n_heads = b_f.shape[0]
    att = n_heads * hd
    n_first = m_first // tm
    m_second = m - m_first
    first = lambda i: (layer, jnp.minimum(i, n_first - 1), 0, 0)
    second = lambda i: (layer, jnp.maximum(i - n_first, 0), 0, 0)
    first3 = lambda i: (layer, jnp.minimum(i, n_first - 1), 0)
    second3 = lambda i: (layer, jnp.maximum(i - n_first, 0), 0)
    kv4 = lambda f: pl.BlockSpec((1, tm, n_heads, hd), f)
    lf3 = lambda f: pl.BlockSpec((1, tm, n_heads), f)
    stacked = lambda rows: jax.ShapeDtypeStruct((n_layers, rows, n_heads, hd), F32)
    out_shape = [jax.ShapeDtypeStruct((m, att), BF16), jax.ShapeDtypeStruct((m, att), BF16),
                 stacked(m_first), stacked(m_first), jax.ShapeDtypeStruct((n_layers, m_first, n_heads), F32),
                 stacked(m_second), stacked(m_second), jax.ShapeDtypeStruct((n_layers, m_second, n_heads), F32)]
    out_specs = [_rows(tm, att), _rows(tm, att), kv4(first), kv4(first), lf3(first3),
                 kv4(second), kv4(second), lf3(second3)]
    in_specs = [_rows(tm, d), _resident((1, d)), _resident((d, n)), _resident((1, hd)), _resident((1, LANES))]
    args = [x, g.reshape(1, d), w, head_gain.reshape(1, hd), jnp.pad(b_f, (0, LANES - n_heads)).reshape(1, LANES)]
    aliases = {}
    if prev is not None:
        in_specs += [pl.BlockSpec(memory_space=pl.ANY)] * 6
        args += list(prev)
        aliases = {5 + t: 2 + t for t in range(6)}
    kern = functools.partial(_kv_proj_kernel, n_heads=n_heads, head_dim=hd, n_first=n_first,
                             aliased=prev is not None)
    return pl.pallas_call(
        kern, name="kv_proj",
        grid=(m // tm,),
        in_specs=in_specs,
        out_specs=out_specs,
        out_shape=out_shape,
        scratch_shapes=[pltpu.VMEM((tm, d), BF16)],
        input_output_aliases=aliases,
        compiler_params=_cparams("arbitrary"),
    )(*args)


def _ple_kernel(x_ref, g_ref, w_ref, pa_ref, pb_ref, wp_ref, o_ref, h_ref, p_ref, *, n_first):
    h_ref[...] = _rms_rows(x_ref[...], g_ref[...]).astype(BF16)
    p_ref[...] = jnp.where(pl.program_id(0) < n_first, pa_ref[...], pb_ref[...]).astype(BF16)
    for c0 in range(0, o_ref.shape[1], COL_CHUNK):
        cs = slice(c0, c0 + COL_CHUNK)
        gate = jax.nn.sigmoid(jnp.dot(h_ref[...], w_ref[:, cs], preferred_element_type=F32))
        proj = jnp.dot(p_ref[...], wp_ref[:, cs], preferred_element_type=F32)
        o_ref[:, cs] = x_ref[:, cs] + gate * proj


def ple_update(x, g, w, p_first, p_second, wp, tm):
    m, d = x.shape
    pd = wp.shape[0]
    n_first = p_first.shape[0] // tm
    return pl.pallas_call(
        functools.partial(_ple_kernel, n_first=n_first), name="ple_update",
        grid=(m // tm,),
        in_specs=[_rows(tm, d), _resident((1, d)), _resident((d, d)), *_split_rows(tm, pd, n_first),
                  _resident((pd, d))],
        out_specs=_rows(tm, d),
        out_shape=jax.ShapeDtypeStruct((m, d), F32),
        scratch_shapes=[pltpu.VMEM((tm, d), BF16), pltpu.VMEM((tm, pd), BF16)],
        compiler_params=_cparams("parallel"),
    )(x, g.reshape(1, d), w, p_first, p_second, wp)


def _out_proj_kernel(aa_ref, ab_ref, w_ref, x_ref, o_ref, a_ref, *, n_first):
    a_ref[...] = jnp.where(pl.program_id(0) < n_first, aa_ref[...], ab_ref[...])
    for c0 in range(0, o_ref.shape[1], COL_CHUNK):
        cs = slice(c0, c0 + COL_CHUNK)
        o_ref[:, cs] = x_ref[:, cs] + jnp.dot(a_ref[...], w_ref[:, cs], preferred_element_type=F32)


def out_proj_residual(a_first, a_second, w, x, tm):
    m, d = x.shape
    k = w.shape[0]
    n_first = a_first.shape[0] // tm
    return pl.pallas_call(
        functools.partial(_out_proj_kernel, n_first=n_first), name="out_proj_residual",
        grid=(m // tm,),
        in_specs=[*_split_rows(tm, k, n_first), _resident((k, d)), _rows(tm, d)],
        out_specs=_rows(tm, d),
        out_shape=jax.ShapeDtypeStruct((m, d), F32),
        scratch_shapes=[pltpu.VMEM((tm, k), BF16)],
        compiler_params=_cparams("parallel"),
    )(a_first, a_second, w, x)


POOL_HIST_ROWS = 16
CONV_HIST_ROWS = 32


def _cp_mixer_kernel(u_ref, hp_ref, hc_ref, pw_ref, ps_ref, cw_ref, cb_ref, lg_ref, lb_ref,
                     mix_ref, npool_ref, nconv_ref, extp_ref, extc_ref, conv_ref, shift_ref,
                     *, tt, width, start_pos, conv_k, row_chunk, col_chunk):
    ti = pl.program_id(1)
    n_t = pl.num_programs(1)
    ph, ch = POOL_HIST_ROWS, CONV_HIST_ROWS
    w = width
    gd = w // len(POOL_WINDOWS)

    @pl.when(ti == 0)
    def _():
        extp_ref[0:ph, :] = hp_ref[0]
        extc_ref[0:ch, :] = hc_ref[0]

    @pl.when(ti > 0)
    def _():
        extp_ref[0:ph, :] = extp_ref[tt:tt + ph, :]
        extc_ref[0:ch, :] = extc_ref[tt:tt + ch, :]

    extp_ref[ph:ph + tt, :] = u_ref[:, 0:w].astype(F32)
    ua = u_ref[:, w:2 * w].astype(F32)
    ub = u_ref[:, 2 * w:3 * w].astype(F32)
    extc_ref[ch:ch + tt, :] = ua * jax.nn.sigmoid(ub)

    for r0 in range(0, tt, row_chunk):
        pos = (start_pos + 1 + r0 + ti * tt
               + lax.broadcasted_iota(jnp.int32, (row_chunk, 1), 0)).astype(F32)
        for g, win in enumerate(POOL_WINDOWS):
            c0 = g * gd
            cur = extp_ref[ph + r0:ph + r0 + row_chunk, c0:c0 + gd]
            acc = cur
            for j in range(1, win):
                acc = acc + extp_ref[ph + r0 - j:ph + r0 - j + row_chunk, c0:c0 + gd]
            pooled = acc / jnp.minimum(pos, float(win)) - cur
            po = jnp.dot(pooled.astype(BF16), pw_ref[g], preferred_element_type=F32)
            mix_ref[r0:r0 + row_chunk, c0:c0 + gd] = (po * ps_ref[:, c0:c0 + gd]).astype(BF16)

    base = ch - (conv_k - 1)
    n_sh = shift_ref.shape[1]
    for s in range(1, SUBLANES):
        shift_ref[s - 1] = extc_ref[s:s + n_sh, :]

    def tap_rows(k, r0, c0):
        off = base + k
        s = off % SUBLANES
        a = off - s + r0
        if s == 0:
            return extc_ref[a:a + row_chunk, c0:c0 + col_chunk]
        return shift_ref[s - 1, a:a + row_chunk, c0:c0 + col_chunk]

    for r0 in range(0, tt, row_chunk):
        for c0 in range(0, w, col_chunk):
            acc = jnp.broadcast_to(cb_ref[:, c0:c0 + col_chunk], (row_chunk, col_chunk))
            for k in range(conv_k):
                acc = acc + cw_ref[k:k + 1, c0:c0 + col_chunk] * tap_rows(k, r0, c0)
            conv_ref[r0:r0 + row_chunk, c0:c0 + col_chunk] = acc

    for r0 in range(0, tt, row_chunk):
        c = conv_ref[r0:r0 + row_chunk, :]
        mu = jnp.mean(c, axis=-1, keepdims=True)
        cc = c - mu
        var = jnp.mean(cc * cc, axis=-1, keepdims=True)
        y = cc * lax.rsqrt(var + NORM_EPS) * lg_ref[...] + lb_ref[...]
        mix_ref[r0:r0 + row_chunk, w:2 * w] = (y * jax.nn.sigmoid(y)).astype(BF16)

    @pl.when(ti == n_t - 1)
    def _():
        npool_ref[0] = extp_ref[ph + tt - (ph - 1):ph + tt, :]
        nconv_ref[0] = extc_ref[ch + tt - (conv_k - 1):ch + tt, :]


def cp_mixer(u, row0, b, t, hist_pool, hist_conv, pool_w, pool_scale, conv_w, conv_b, ln_g, ln_b, start_pos):
    w = u.shape[1] // 3
    conv_k = conv_w.shape[0]
    n_pool_hist = hist_pool.shape[1]
    assert n_pool_hist == POOL_HIST_ROWS - 1 and conv_k - 1 <= CONV_HIST_ROWS
    tt = _pick(t, (256, 128, 64, 32, 16))
    assert row0 % tt == 0
    n_t = t // tt
    blk0 = row0 // tt
    hp = jnp.pad(hist_pool, ((0, 0), (POOL_HIST_ROWS - n_pool_hist, 0), (0, 0)))
    hc = jnp.pad(hist_conv, ((0, 0), (CONV_HIST_ROWS - (conv_k - 1), 0), (0, 0)))
    kern = functools.partial(_cp_mixer_kernel, tt=tt, width=w, start_pos=start_pos, conv_k=conv_k,
                             row_chunk=min(tt, 64), col_chunk=2 * LANES)
    n_g, gd = pool_w.shape[0], pool_w.shape[1]
    vec = lambda: pl.BlockSpec((1, w), lambda i, j: (0, 0))
    return pl.pallas_call(
        kern, name="cp_mixer",
        grid=(b, n_t),
        in_specs=[
            pl.BlockSpec((tt, 3 * w), lambda i, j: (blk0 + i * n_t + j, 0)),
            pl.BlockSpec((1, POOL_HIST_ROWS, w), lambda i, j: (i, 0, 0)),
            pl.BlockSpec((1, CONV_HIST_ROWS, w), lambda i, j: (i, 0, 0)),
            pl.BlockSpec((n_g, gd, gd), lambda i, j: (0, 0, 0)),
            vec(),
            pl.BlockSpec((conv_k, w), lambda i, j: (0, 0)),
            vec(), vec(), vec(),
        ],
        out_specs=[
            pl.BlockSpec((tt, 2 * w), lambda i, j: (i * n_t + j, 0)),
            pl.BlockSpec((1, n_pool_hist, w), lambda i, j: (i, 0, 0)),
            pl.BlockSpec((1, conv_k - 1, w), lambda i, j: (i, 0, 0)),
        ],
        out_shape=[
            jax.ShapeDtypeStruct((b * t, 2 * w), BF16),
            jax.ShapeDtypeStruct((b, n_pool_hist, w), F32),
            jax.ShapeDtypeStruct((b, conv_k - 1, w), F32),
        ],
        scratch_shapes=[
            pltpu.VMEM((POOL_HIST_ROWS + tt, w), F32),
            pltpu.VMEM((CONV_HIST_ROWS + tt, w), F32),
            pltpu.VMEM((tt, w), F32),
            pltpu.VMEM((SUBLANES - 1, CONV_HIST_ROWS + tt - SUBLANES, w), F32),
        ],
        compiler_params=_cparams("arbitrary", "arbitrary"),
    )(u, hp, hc, pool_w.astype(BF16), pool_scale.reshape(1, w), conv_w, conv_b.reshape(1, w),
      ln_g.reshape(1, w), ln_b.reshape(1, w))


FLASH_HEADS_PER_STEP = 2
FLASH_BLOCK = 512


def _flash_kernel(q_ref, k_ref, v_ref, lf_ref, o_ref, vt_ref, frow_ref, fcol_ref, *, t, n_t, head_dim, n_hp):
    s_len = t * n_t
    hd = head_dim
    n_c = s_len // LANES
    upper = (lax.broadcasted_iota(jnp.int32, (LANES, LANES), 0)
             <= lax.broadcasted_iota(jnp.int32, (LANES, LANES), 1)).astype(F32)
    earlier = (lax.broadcasted_iota(jnp.int32, (n_c, n_c), 1)
               < lax.broadcasted_iota(jnp.int32, (n_c, n_c), 0)).astype(F32)
    for hh in range(n_hp):
        for c in range(0, s_len, t):
            vt_ref[hh, :, c:c + t] = jnp.transpose(v_ref[c:c + t, hh * hd:(hh + 1) * hd].astype(F32)).astype(BF16)
        within = jnp.dot(lf_ref[0, hh], upper, preferred_element_type=F32, precision=lax.Precision.HIGHEST)
        totals = jnp.broadcast_to(within[:, LANES - 1:LANES], (n_c, LANES))
        f_all = (within + jnp.dot(earlier, totals, preferred_element_type=F32,
                                  precision=lax.Precision.HIGHEST)) * LOG2E
        frow_ref[hh] = f_all
        for c in range(n_c):
            fcol_ref[hh, c * LANES:(c + 1) * LANES, :] = jnp.transpose(
                jnp.broadcast_to(f_all[c:c + 1, :], (LANES, LANES)))

    below_diag = (lax.broadcasted_iota(jnp.int32, (t, t), 0) <= lax.broadcasted_iota(jnp.int32, (t, t), 1))

    def q_block(i, _):
        q0 = pl.multiple_of(i * t, t)
        qs = [q_ref[pl.ds(q0, t), hh * hd:(hh + 1) * hd] for hh in range(n_hp)]
        c0 = i * (t // LANES)
        fqs = [jnp.concatenate([frow_ref[hh, pl.ds(c0 + r, 1), :] for r in range(t // LANES)], axis=1)
               for hh in range(n_hp)]

        def step(j, carry, masked):
            k0 = pl.multiple_of(j * t, t)
            out = []
            for hh in range(n_hp):
                m, l, acc = carry[hh]
                k = k_ref[pl.ds(k0, t), hh * hd:(hh + 1) * hd]
                f_k = fcol_ref[hh, pl.ds(k0, t), :]
                s = lax.dot_general(k, qs[hh], _NT, preferred_element_type=F32)
                s = s + (fqs[hh] - jnp.tile(f_k, (1, t // LANES)))
                if masked:
                    s = jnp.where(below_diag, s, NEG_INF)
                m_new = jnp.maximum(m, jnp.max(s, axis=0, keepdims=True))
                alpha = jnp.exp2(m - m_new)
                p = jnp.exp2(s - m_new)
                l_new = alpha * l + jnp.sum(p, axis=0, keepdims=True)
                pv = jnp.dot(vt_ref[hh, :, pl.ds(k0, t)], p.astype(BF16), preferred_element_type=F32)
                out.append((m_new, l_new, alpha * acc + pv))
            return tuple(out)

        init = tuple((jnp.full((1, t), NEG_INF, F32), jnp.zeros((1, t), F32), jnp.zeros((hd, t), F32))
                     for _ in range(n_hp))
        carry = lax.fori_loop(0, i, lambda j, c: step(j, c, False), init)
        carry = step(i, carry, True)
        for hh in range(n_hp):
            _, l, acc = carry[hh]
            o_ref[pl.ds(q0, t), hh * hd:(hh + 1) * hd] = jnp.transpose(acc / l).astype(o_ref.dtype)
        return 0

    lax.fori_loop(0, n_t, q_block, 0)


def flash_prompt(q, k, v, logf, b, s_len, *, n_heads, head_dim):
    n_hp = FLASH_HEADS_PER_STEP
    t = _pick(s_len, (FLASH_BLOCK, 256, 128))
    n_c = s_len // LANES
    kern = functools.partial(_flash_kernel, t=t, n_t=s_len // t, head_dim=head_dim, n_hp=n_hp)
    heads = lambda: pl.BlockSpec((s_len, n_hp * head_dim), lambda i, h: (i, h))
    return pl.pallas_call(
        kern, name="flash_prompt",
        grid=(b, n_heads // n_hp),
        in_specs=[heads(), heads(), heads(), pl.BlockSpec((1, n_hp, n_c, LANES), lambda i, h: (i, h, 0, 0))],
        out_specs=heads(),
        out_shape=jax.ShapeDtypeStruct((b * s_len, n_heads * head_dim), BF16),
        scratch_shapes=[pltpu.VMEM((n_hp, head_dim, s_len), BF16), pltpu.VMEM((n_hp, n_c, LANES), F32),
                        pltpu.VMEM((n_hp, s_len, LANES), F32)],
        compiler_params=_cparams("parallel", "parallel"),
    )(q, k, v, logf)


def _sample_attn_kernel(q_ref, kn_ref, vn_ref, kc_ref, vc_ref, fp_ref, fn_ref, o_ref,
                        qbd_ref, m_ref, l_ref, acc_ref, kmat_ref, vmat_ref, *, t, n_heads, head_dim):
    j = pl.program_id(1)
    rows = 64

    def relayout(c, _):
        r0 = pl.multiple_of(c * rows, rows)
        for h in range(n_heads):
            kmat_ref[pl.ds(r0, rows), h * head_dim:(h + 1) * head_dim] = kc_ref[0, 0, pl.ds(r0, rows), h, :]
            vmat_ref[pl.ds(r0, rows), h * head_dim:(h + 1) * head_dim] = vc_ref[0, 0, pl.ds(r0, rows), h, :]
        return 0

    lax.fori_loop(0, kmat_ref.shape[0] // rows, relayout, 0)
    n_j = pl.num_programs(1)
    r = n_heads * t
    hd = n_heads * head_dim

    def same_head():
        return (lax.broadcasted_iota(jnp.int32, (r, hd), 0) // t
                == lax.broadcasted_iota(jnp.int32, (r, hd), 1) // head_dim)

    def rep_rows(x):
        return jnp.broadcast_to(x[:, None, :], (n_heads, t, x.shape[-1])).reshape(r, x.shape[-1])

    f_new = rep_rows(fn_ref[0]) * LOG2E
    row_t = lax.broadcasted_iota(jnp.int32, (r, t), 0) % t
    col_t = lax.broadcasted_iota(jnp.int32, (r, t), 1)
    f_q = jnp.sum(jnp.where(col_t == row_t, f_new, 0.0), axis=-1, keepdims=True)

    @pl.when(j == 0)
    def _():
        q_rep = jnp.broadcast_to(q_ref[...][None], (n_heads, t, hd)).reshape(r, hd)
        qbd_ref[...] = jnp.where(same_head(), q_rep, jnp.zeros_like(q_rep))
        m_ref[...] = jnp.full(m_ref.shape, NEG_INF, F32)
        l_ref[...] = jnp.zeros(l_ref.shape, F32)
        acc_ref[...] = jnp.zeros(acc_ref.shape, F32)

    def update(k, v, bias, valid):
        s = lax.dot_general(qbd_ref[...], k, _NT, preferred_element_type=F32) + bias
        if valid is not None:
            s = jnp.where(valid, s, NEG_INF)
        m = m_ref[...]
        m_new = jnp.maximum(m, jnp.max(s, axis=-1, keepdims=True))
        alpha = jnp.exp2(m - m_new)
        p = jnp.exp2(s - m_new)
        l_ref[...] = alpha * l_ref[...] + jnp.sum(p, axis=-1, keepdims=True)
        acc_ref[...] = alpha * acc_ref[...] + jnp.dot(p.astype(BF16), v, preferred_element_type=F32)
        m_ref[...] = m_new

    update(kmat_ref[...].astype(BF16), vmat_ref[...].astype(BF16), f_q - rep_rows(fp_ref[0]) * LOG2E, None)

    @pl.when(j == n_j - 1)
    def _():
        update(kn_ref[...], vn_ref[...], f_q - f_new, col_t <= row_t)
        o_full = jnp.where(same_head(), acc_ref[...] / l_ref[...], 0.0)
        o = o_full[0:t]
        for h in range(1, n_heads):
            o = o + o_full[h * t:(h + 1) * t]
        o_ref[...] = o.astype(o_ref.dtype)


def sample_attention(q, k, v, row0, b, t, cache_k, cache_v, layer, f_past, f_new, *, n_heads, head_dim):
    hd = n_heads * head_dim
    p = cache_k.shape[2]
    tp = _pick(p, (512, 256, 128))
    r = n_heads * t
    assert row0 % t == 0
    blk0 = row0 // t
    kern = functools.partial(_sample_attn_kernel, t=t, n_heads=n_heads, head_dim=head_dim)
    new = lambda: pl.BlockSpec((t, hd), lambda i, j: (blk0 + i, 0))
    past = lambda: pl.BlockSpec((1, 1, tp, n_heads, head_dim), lambda i, j: (layer, i, j, 0, 0))
    return pl.pallas_call(
        kern, name="sample_attention",
        grid=(b, p // tp),
        in_specs=[new(), new(), new(), past(), past(),
                  pl.BlockSpec((1, n_heads, tp), lambda i, j: (i, 0, j)),
                  pl.BlockSpec((1, n_heads, t), lambda i, j: (i, 0, 0))],
        out_specs=pl.BlockSpec((t, hd), lambda i, j: (i, 0)),
        out_shape=jax.ShapeDtypeStruct((b * t, hd), BF16),
        scratch_shapes=[pltpu.VMEM((r, hd), BF16), pltpu.VMEM((r, 1), F32), pltpu.VMEM((r, 1), F32),
                        pltpu.VMEM((r, hd), F32), pltpu.VMEM((tp, hd), F32), pltpu.VMEM((tp, hd), F32)],
        compiler_params=_cparams("parallel", "arbitrary"),
    )(q, k, v, cache_k, cache_v, f_past, f_new)


def _router_kernel(x_ref, g_ref, w_ref, b_ref, ids_ref, wts_ref, cnt_ref, seen_ref, *, n_groups, per_group):
    @pl.when(pl.program_id(0) == 0)
    def _():
        seen_ref[...] = jnp.zeros(seen_ref.shape, F32)

    h = _rms_rows(x_ref[...], g_ref[...])
    logits = jnp.dot(h, w_ref[...], preferred_element_type=F32,
                     precision=lax.Precision.HIGHEST) + b_ref[...]
    tm, n = logits.shape
    col = lax.broadcasted_iota(jnp.int32, (tm, n), 1)
    neg = jnp.float32(-jnp.inf)

    def first_max(vals):
        top = jnp.max(vals, axis=-1, keepdims=True)
        idx = jnp.min(jnp.where(vals == top, col, n), axis=-1, keepdims=True)
        return top, idx

    gl = jnp.where(col < n_groups, logits, neg)
    g_top, g_sel = first_max(gl)
    g_prob = 1.0 / jnp.sum(jnp.exp(gl - g_top), axis=-1, keepdims=True)
    lo = n_groups + g_sel * per_group
    el = jnp.where((col >= lo) & (col < lo + per_group), logits, neg)
    v1, i1 = first_max(el)
    v2, i2 = first_max(jnp.where(col == i1, neg, el))
    e2 = jnp.exp(v2 - v1)
    w1 = g_prob / (1.0 + e2)
    w2 = g_prob * e2 / (1.0 + e2)
    e1 = i1 - n_groups
    e2 = i2 - n_groups
    chosen = jnp.where((col == e1) | (col == e2), 1.0, 0.0)
    earlier_rows = (lax.broadcasted_iota(jnp.int32, (tm, tm), 1)
                    < lax.broadcasted_iota(jnp.int32, (tm, tm), 0))
    before = seen_ref[...] + jnp.dot(jnp.where(earlier_rows, 1.0, 0.0).astype(BF16), chosen.astype(BF16),
                                     preferred_element_type=F32)
    r1 = jnp.sum(jnp.where(col == e1, before, 0.0), axis=-1, keepdims=True).astype(jnp.int32)
    r2 = jnp.sum(jnp.where(col == e2, before, 0.0), axis=-1, keepdims=True).astype(jnp.int32)
    seen_ref[...] = seen_ref[...] + jnp.sum(chosen, axis=0, keepdims=True)
    cnt_ref[...] = seen_ref[...].astype(jnp.int32)
    ids_ref[...] = jnp.where(col == 0, e1, jnp.where(col == 1, e2, jnp.where(col == 2, r1,
                                                                             jnp.where(col == 3, r2, 0))))
    wts_ref[...] = jnp.where(col == 0, w1, jnp.where(col == 1, w2, 0.0))


def router(x, g, w_pad, b_pad, *, n_groups, per_group):
    m, d = x.shape
    n = w_pad.shape[1]
    tm = _pick(m, (256, 128, 64, 32, 16, 8))
    kern = functools.partial(_router_kernel, n_groups=n_groups, per_group=per_group)
    return pl.pallas_call(
        kern, name="router",
        grid=(m // tm,),
        in_specs=[_rows(tm, d), _resident((1, d)), _resident((d, n)), _resident((1, n))],
        out_specs=[_rows(tm, n), _rows(tm, n), _resident((1, n))],
        out_shape=[jax.ShapeDtypeStruct((m, n), jnp.int32), jax.ShapeDtypeStruct((m, n), F32),
                   jax.ShapeDtypeStruct((1, n), jnp.int32)],
        scratch_shapes=[pltpu.VMEM((1, n), F32)],
        compiler_params=_cparams("arbitrary"),
    )(x, g.reshape(1, d), w_pad, b_pad.reshape(1, n))


EXPERT_BLOCK_ROWS = 256


def _row_copy(src_ref, src_row, dst_ref, dst_row, sem):
    return pltpu.make_async_copy(src_ref.at[pl.ds(src_row, 1)], dst_ref.at[pl.ds(dst_row, 1)], sem)


SEL_FIELDS = 2 * TOP_K
DMA_UNROLL = 8


def _dest_row(start_ref, sel_ref, r, k):
    return start_ref[sel_ref[0, 0, SEL_FIELDS * r + k]] + sel_ref[0, 0, SEL_FIELDS * r + TOP_K + k]


def _sel_blocks(ids, tm):
    m = ids.shape[0]
    return ids[:, :SEL_FIELDS].reshape(m // tm, 1, tm * SEL_FIELDS)


def _dispatch_kernel(start_ref, sel_ref, x_ref, buf_in_ref, buf_ref, sem, *, tm):
    del buf_in_ref

    def copies(r):
        return [_row_copy(x_ref, r, buf_ref, _dest_row(start_ref, sel_ref, r, k), sem) for k in range(TOP_K)]

    def start(r, _):
        for c in copies(r):
            c.start()
        return 0

    def wait(r, _):
        for c in copies(r):
            c.wait()
        return 0

    lax.fori_loop(0, tm, start, 0, unroll=DMA_UNROLL)
    lax.fori_loop(0, tm, wait, 0, unroll=DMA_UNROLL)


def dispatch_rows(x, ids, pad_start, buf):
    m, d = x.shape
    tm = _pick(m, (256, 128, 64, 32, 16, 8))
    return pl.pallas_call(
        functools.partial(_dispatch_kernel, tm=tm), name="moe_dispatch",
        grid_spec=pltpu.PrefetchScalarGridSpec(
            num_scalar_prefetch=1,
            grid=(m // tm,),
            in_specs=[pl.BlockSpec((1, 1, tm * SEL_FIELDS), lambda i, st: (i, 0, 0), memory_space=pltpu.SMEM),
                      pl.BlockSpec((tm, d), lambda i, st: (i, 0)),
                      pl.BlockSpec(memory_space=pl.ANY)],
            out_specs=pl.BlockSpec(memory_space=pl.ANY),
            scratch_shapes=[pltpu.SemaphoreType.DMA(())],
        ),
        out_shape=jax.ShapeDtypeStruct(buf.shape, buf.dtype),
        input_output_aliases={3: 0},
        compiler_params=_cparams("arbitrary"),
    )(pad_start, _sel_blocks(ids, tm), x, buf)


def _expert_kernel(be_ref, nu_ref, x_ref, g_ref, wg_ref, wu_ref, wd_ref, o_ref, wg_b, wu_b, wd_b):
    i = pl.program_id(0)
    used = i < nu_ref[0]

    @pl.when(used & ((i == 0) | (be_ref[i] != be_ref[jnp.maximum(i - 1, 0)])))
    def _():
        wg_b[...] = wg_ref[0, 0].astype(BF16)
        wu_b[...] = wu_ref[0, 0].astype(BF16)
        wd_b[...] = wd_ref[0, 0].astype(BF16)

    @pl.when(used)
    def _():
        h = _rms_rows(x_ref[...], g_ref[...]).astype(BF16)
        gate = jnp.dot(h, wg_b[...], preferred_element_type=F32)
        up = jnp.dot(h, wu_b[...], preferred_element_type=F32)
        act = (gate * jax.nn.sigmoid(gate) * up).astype(BF16)
        o_ref[...] = jnp.dot(act, wd_b[...], preferred_element_type=F32)

    @pl.when(jnp.logical_not(used))
    def _():
        o_ref[...] = jnp.zeros(o_ref.shape, o_ref.dtype)


def expert_mlps(buf, g, w_gate, w_up, w_down, layer, block_expert, n_used):
    rows, d = buf.shape
    de = w_gate.shape[3]
    br = EXPERT_BLOCK_ROWS
    nb = rows // br
    blk = lambda i, be, nu: (jnp.minimum(i, nu[0] - 1), 0)
    wsel = lambda i, be, nu: (layer, be[i], 0, 0)
    return pl.pallas_call(
        _expert_kernel, name="expert_mlps",
        grid_spec=pltpu.PrefetchScalarGridSpec(
            num_scalar_prefetch=2,
            grid=(nb,),
            in_specs=[pl.BlockSpec((br, d), blk),
                      pl.BlockSpec((1, d), lambda i, be, nu: (0, 0)),
                      pl.BlockSpec((1, 1, d, de), wsel),
                      pl.BlockSpec((1, 1, d, de), wsel),
                      pl.BlockSpec((1, 1, de, d), wsel)],
            out_specs=pl.BlockSpec((br, d), lambda i, be, nu: (i, 0)),
            scratch_shapes=[pltpu.VMEM((d, de), BF16), pltpu.VMEM((d, de), BF16), pltpu.VMEM((de, d), BF16)],
        ),
        out_shape=jax.ShapeDtypeStruct((rows, d), F32),
        compiler_params=_cparams("arbitrary"),
    )(block_expert, n_used, buf, g.reshape(1, d), w_gate, w_up, w_down)


def _combine_kernel(start_ref, sel_cur_ref, sel_nxt_ref, x_ref, wts_ref, y_ref, o_ref, rows_ref, sems, *, tm):
    i = pl.program_id(0)
    n = pl.num_programs(0)

    def copies(sel_ref, slot, r):
        return [_row_copy(y_ref, _dest_row(start_ref, sel_ref, r, k), rows_ref.at[slot, k], r, sems.at[slot])
                for k in range(TOP_K)]

    def start_all(sel_ref, slot):
        def body(r, _):
            for c in copies(sel_ref, slot, r):
                c.start()
            return 0
        lax.fori_loop(0, tm, body, 0, unroll=DMA_UNROLL)

    @pl.when(i == 0)
    def _():
        start_all(sel_cur_ref, 0)

    @pl.when(i + 1 < n)
    def _():
        start_all(sel_nxt_ref, (i + 1) % 2)

    slot = i % 2

    def wait_body(r, _):
        for c in copies(sel_cur_ref, slot, r):
            c.wait()
        return 0
    lax.fori_loop(0, tm, wait_body, 0, unroll=DMA_UNROLL)

    wts = wts_ref[...]
    o_ref[...] = (x_ref[...] + wts[:, 0:1] * rows_ref[slot, 0] + wts[:, 1:2] * rows_ref[slot, 1])


def combine_rows(x, wts, y, ids, pad_start):
    m, d = x.shape
    tm = _pick(m, (256, 128, 64, 32, 16, 8))
    n_t = m // tm
    sel = _sel_blocks(ids, tm)
    smem = lambda f: pl.BlockSpec((1, 1, tm * SEL_FIELDS), f, memory_space=pltpu.SMEM)
    return pl.pallas_call(
        functools.partial(_combine_kernel, tm=tm), name="moe_combine",
        grid_spec=pltpu.PrefetchScalarGridSpec(
            num_scalar_prefetch=1,
            grid=(n_t,),
            in_specs=[smem(lambda i, st: (i, 0, 0)),
                      smem(lambda i, st: (jnp.minimum(i + 1, n_t - 1), 0, 0)),
                      pl.BlockSpec((tm, d), lambda i, st: (i, 0)),
                      pl.BlockSpec((tm, wts.shape[1]), lambda i, st: (i, 0)),
                      pl.BlockSpec(memory_space=pl.ANY)],
            out_specs=pl.BlockSpec((tm, d), lambda i, st: (i, 0)),
            scratch_shapes=[pltpu.VMEM((2, TOP_K, tm, d), F32), pltpu.SemaphoreType.DMA((2,))],
        ),
        out_shape=jax.ShapeDtypeStruct((m, d), F32),
        compiler_params=_cparams("arbitrary"),
    )(pad_start, sel, sel, x, wts, y)


def _block_tables(counts, block_rows, n_blocks):
    n_experts = counts.shape[0]
    padded = (counts + block_rows - 1) // block_rows * block_rows
    pad_end = jnp.cumsum(padded)
    pad_start = (pad_end - padded).astype(jnp.int32)
    block_expert = jnp.minimum(
        jnp.searchsorted(pad_end, jnp.arange(n_blocks, dtype=jnp.int32) * block_rows, side='right'),
        n_experts - 1).astype(jnp.int32)
    n_used = (pad_end[-1:] // block_rows).astype(jnp.int32)
    return pad_start, block_expert, n_used


def moe_buffer_rows(m, n_experts):
    return (-(-m * TOP_K // EXPERT_BLOCK_ROWS) + n_experts) * EXPERT_BLOCK_ROWS


def hier_moe(x, buf, g, w_router, b_router, w_gate, w_up, w_down, layer, *, n_groups, per_group):
    n_experts = n_groups * per_group
    ids, wts, counts = router(x, g, w_router, b_router, n_groups=n_groups, per_group=per_group)
    pad_start, block_expert, n_used = _block_tables(counts[0, :n_experts], EXPERT_BLOCK_ROWS,
                                                    buf.shape[0] // EXPERT_BLOCK_ROWS)
    buf = dispatch_rows(x, ids, pad_start, buf)
    y = expert_mlps(buf, g, w_gate, w_up, w_down, layer, block_expert, n_used)
    return combine_rows(x, wts, y, ids, pad_start), buf


def _pad_cols(w, n):
    return jnp.pad(w, ((0, 0), (0, n - w.shape[1])))


def kernel(x_prompt, x_sample, cache_k, cache_v, cache_logf, state_pool, state_conv, p_prompt, p_sample,
           norm_mix, norm_ffn, norm_ple, w_in_cp, pool_w, pool_scale, conv_w, conv_b, conv_ln_g, conv_ln_b,
           w_out_cp, w_in_fox, b_forget, q_norm, k_norm, w_out_fox, router_group_w, router_group_b,
           router_expert_w, router_expert_b, expert_w_gate, expert_w_up, expert_w_down, w_ple_gate, w_ple_proj):
    bp, sp, d = x_prompt.shape
    bs, ts, _ = x_sample.shape
    depth = norm_mix.shape[0]
    n_fox = w_in_fox.shape[0]
    n_heads = b_forget.shape[1]
    head_dim = q_norm.shape[1]
    att = n_heads * head_dim
    past = cache_k.shape[2]
    n_groups = router_group_w.shape[2]
    n_experts = router_expert_w.shape[2]
    per_group = n_experts // n_groups
    width = pool_w.shape[1] * pool_w.shape[2]
    q_scale = float(head_dim) ** -0.5 * LOG2E

    mp, ms = bp * sp, bs * ts
    tm = _row_tile(mp, ms)
    x = jnp.concatenate([x_prompt.reshape(mp, d), x_sample.reshape(ms, d)], axis=0)
    pools_p, convs_p, pools_s, convs_s = [], [], [], []
    fox_out = None
    moe_buf = jnp.zeros((moe_buffer_rows(mp + ms, n_experts), d), F32)

    for i in range(depth):
        j = i // 2
        if i % 2 == 0:
            u = cp_in_proj(x, norm_mix[i], w_in_cp[j].astype(BF16), tm)
            mixer = functools.partial(cp_mixer, pool_w=pool_w[j], pool_scale=pool_scale[j], conv_w=conv_w[j],
                                      conv_b=conv_b[j], ln_g=conv_ln_g[j], ln_b=conv_ln_b[j])
            mix_p, n_pool, n_conv = mixer(u, 0, bp, sp, jnp.zeros((bp,) + state_pool.shape[2:], F32),
                                          jnp.zeros((bp,) + state_conv.shape[2:], F32), start_pos=0)
            pools_p.append(n_pool)
            convs_p.append(n_conv)
            mix_s, n_pool, n_conv = mixer(u, mp, bs, ts, state_pool[j], state_conv[j], start_pos=past)
            pools_s.append(n_pool)
            convs_s.append(n_conv)
            x = out_proj_residual(mix_p, mix_s, w_out_cp[j].astype(BF16), x, tm)
        else:
            w_in = w_in_fox[j]
            g = norm_mix[i]
            q = q_proj(x, g, w_in[:, :att].astype(BF16), q_norm[j], q_scale, tm)
            w_kvf = _pad_cols(w_in[:, att:], 2 * att + LANES).astype(BF16)
            res = kv_proj(x, g, w_kvf, k_norm[j], b_forget[j], j, n_fox, mp, fox_out, min(tm, 256))
            k16, v16 = res[0], res[1]
            fox_out = res[2:]
            lf_p = fox_out[2][j].reshape(bp, sp, n_heads)
            lf_s = fox_out[5][j].reshape(bs, ts, n_heads)
            o_p = flash_prompt(q, k16, v16, lf_p.transpose(0, 2, 1).reshape(bp, n_heads, sp // LANES, LANES),
                               bp, sp, n_heads=n_heads, head_dim=head_dim)
            f_all = jnp.cumsum(jnp.concatenate([cache_logf[j], lf_s], axis=1), axis=1).transpose(0, 2, 1)
            o_s = sample_attention(q, k16, v16, mp, bs, ts, cache_k, cache_v, j, f_all[:, :, :past],
                                   f_all[:, :, past:], n_heads=n_heads, head_dim=head_dim)
            x = out_proj_residual(o_p, o_s, w_out_fox[j].astype(BF16), x, tm)

        w_router = _pad_cols(jnp.concatenate([router_group_w[i], router_expert_w[i]], axis=1), LANES)
        b_router = jnp.pad(jnp.concatenate([router_group_b[i], router_expert_b[i]]),
                           (0, LANES - n_groups - n_experts))
        x, moe_buf = hier_moe(x, moe_buf, norm_ffn[i], w_router, b_router, expert_w_gate, expert_w_up,
                              expert_w_down, i, n_groups=n_groups, per_group=per_group)
        x = ple_update(x, norm_ple[i], w_ple_gate[i].astype(BF16), p_prompt[i].reshape(mp, -1),
                       p_sample[i].reshape(ms, -1), w_ple_proj[i].astype(BF16), tm)

    k4p, v4p, lfp, k4s, v4s, lfs = fox_out
    return (x[:mp].reshape(bp, sp, d), x[mp:].reshape(bs, ts, d),
            k4p.reshape(n_fox, bp, sp, n_heads, head_dim), v4p.reshape(n_fox, bp, sp, n_heads, head_dim),
            lfp.reshape(n_fox, bp, sp, n_heads), jnp.stack(pools_p), jnp.stack(convs_p),
            k4s.reshape(n_fox, bs, ts, n_heads, head_dim), v4s.reshape(n_fox, bs, ts, n_heads, head_dim),
            lfs.reshape(n_fox, bs, ts, n_heads), jnp.stack(pools_s), jnp.stack(convs_s))
```

```python
import functools
import math

import jax
import jax.numpy as jnp
from jax import lax
from jax.experimental import pallas as pl
from jax.experimental.pallas import tpu as pltpu

F32 = jnp.float32
BF16 = jnp.bfloat16

NORM_EPS = 1e-6
NEG_INF = -1e30
LOG2E = math.log2(math.e)
POOL_WINDOWS = (2, 4, 8, 16)
TOP_K = 2

V7X_VMEM_BYTES = 64 * 1024 * 1024
VMEM_LIMIT = V7X_VMEM_BYTES - 8 * 1024 * 1024
LANES = 128
SUBLANES = 8

ROW_TILE = 512
COL_CHUNK = 512
_NT = (((1,), (1,)), ((), ()))


def _cparams(*sem):
    return pltpu.CompilerParams(dimension_semantics=sem, vmem_limit_bytes=VMEM_LIMIT)


def _pick(n, prefs):
    for p in prefs:
        if n % p == 0:
            return p
    return n


def _rms_rows(x, g):
    ms = jnp.mean(x * x, axis=-1, keepdims=True)
    return x * lax.rsqrt(ms + NORM_EPS) * g


def _resident(shape):
    zeros = (0,) * len(shape)
    return pl.BlockSpec(shape, lambda i: zeros)


def _rows(tm, width):
    return pl.BlockSpec((tm, width), lambda i: (i, 0))


def _split_rows(tm, width, n_first):
    return (pl.BlockSpec((tm, width), lambda i: (jnp.minimum(i, n_first - 1), 0)),
            pl.BlockSpec((tm, width), lambda i: (jnp.maximum(i - n_first, 0), 0)))


def _row_tile(m_first, m_second):
    return _pick(math.gcd(m_first, m_second), (ROW_TILE, 256, 128, 64, 32, 16))


def _cp_in_kernel(x_ref, g_ref, w_ref, o_ref, h_ref):
    h_ref[...] = _rms_rows(x_ref[...], g_ref[...]).astype(BF16)
    for c0 in range(0, o_ref.shape[1], COL_CHUNK):
        o_ref[:, c0:c0 + COL_CHUNK] = jnp.dot(
            h_ref[...], w_ref[:, c0:c0 + COL_CHUNK], preferred_element_type=F32).astype(o_ref.dtype)


def cp_in_proj(x, g, w, tm):
    m, d = x.shape
    n = w.shape[1]
    return pl.pallas_call(
        _cp_in_kernel, name="cp_in_proj",
        grid=(m // tm,),
        in_specs=[_rows(tm, d), _resident((1, d)), _resident((d, n))],
        out_specs=_rows(tm, n),
        out_shape=jax.ShapeDtypeStruct((m, n), BF16),
        scratch_shapes=[pltpu.VMEM((tm, d), BF16)],
        compiler_params=_cparams("parallel"),
    )(x, g.reshape(1, d), w)


def _q_proj_kernel(x_ref, g_ref, w_ref, hg_ref, o_ref, h_ref, *, head_dim, scale):
    h_ref[...] = _rms_rows(x_ref[...], g_ref[...]).astype(BF16)
    for c0 in range(0, o_ref.shape[1], COL_CHUNK):
        acc = jnp.dot(h_ref[...], w_ref[:, c0:c0 + COL_CHUNK], preferred_element_type=F32)
        for h0 in range(0, COL_CHUNK, head_dim):
            a = _rms_rows(acc[:, h0:h0 + head_dim], hg_ref[...]) * scale
            o_ref[:, c0 + h0:c0 + h0 + head_dim] = a.astype(BF16)


def q_proj(x, g, w, head_gain, scale, tm):
    m, d = x.shape
    n = w.shape[1]
    hd = head_gain.shape[0]
    return pl.pallas_call(
        functools.partial(_q_proj_kernel, head_dim=hd, scale=scale), name="q_proj",
        grid=(m // tm,),
        in_specs=[_rows(tm, d), _resident((1, d)), _resident((d, n)), _resident((1, hd))],
        out_specs=_rows(tm, n),
        out_shape=jax.ShapeDtypeStruct((m, n), BF16),
        scratch_shapes=[pltpu.VMEM((tm, d), BF16)],
        compiler_params=_cparams("parallel"),
    )(x, g.reshape(1, d), w, head_gain.reshape(1, hd))


def _kv_proj_kernel(x_ref, g_ref, w_ref, hg_ref, bf_ref, *rest, n_heads, head_dim, n_first, aliased):
    if aliased:
        rest = rest[6:]
    k_ref, v_ref, k4p_ref, v4p_ref, lfp_ref, k4s_ref, v4s_ref, lfs_ref, h_ref = rest
    att = n_heads * head_dim
    h_ref[...] = _rms_rows(x_ref[...], g_ref[...]).astype(BF16)
    i = pl.program_id(0)

    def emit(k4_ref, v4_ref, lf_ref):
        for c0 in range(0, 2 * att, COL_CHUNK):
            acc = jnp.dot(h_ref[...], w_ref[:, c0:c0 + COL_CHUNK], preferred_element_type=F32)
            for h0 in range(0, COL_CHUNK, head_dim):
                col = c0 + h0
                head = (col % att) // head_dim
                a = acc[:, h0:h0 + head_dim]
                if col < att:
                    a = _rms_rows(a, hg_ref[...])
                    k_ref[:, col:col + head_dim] = a.astype(BF16)
                    k4_ref[0, :, head, :] = a
                else:
                    v_ref[:, col - att:col - att + head_dim] = a.astype(BF16)
                    v4_ref[0, :, head, :] = a
        z = jnp.dot(h_ref[...], w_ref[:, 2 * att:2 * att + LANES], preferred_element_type=F32) + bf_ref[...]
        logf = jnp.minimum(z, 0.0) - jnp.log1p(jnp.exp(-jnp.abs(z)))
        lf_ref[0] = logf[:, 0:n_heads]

    @pl.when(i < n_first)
    def _():
        emit(k4p_ref, v4p_ref, lfp_ref)

    @pl.when(i >= n_first)
    def _():
        emit(k4s_ref, v4s_ref, lfs_ref)


def kv_proj(x, g, w, head_gain, b_f, layer, n_layers, m_first, prev, tm):
    m, d = x.shape
    n = w.shape[1]
    hd = head_gain.shape[0]
    n_heads = b_f.shape[0]
    att = n_heads * hd
    n_first = m_first // tm
    m_second = m - m_first
    first = lambda i: (layer, jnp.minimum(i, n_first - 1), 0, 0)
    second = lambda i: (layer, jnp.maximum(i - n_first, 0), 0, 0)
    first3 = lambda i: (layer, jnp.minimum(i, n_first - 1), 0)
    second3 = lambda i: (layer, jnp.maximum(i - n_first, 0), 0)
    kv4 = lambda f: pl.BlockSpec((1, tm, n_heads, hd), f)
    lf3 = lambda f: pl.BlockSpec((1, tm, n_heads), f)
    stacked = lambda rows: jax.ShapeDtypeStruct((n_layers, rows, n_heads, hd), F32)
    out_shape = [jax.ShapeDtypeStruct((m, att), BF16), jax.ShapeDtypeStruct((m, att), BF16),
                 stacked(m_first), stacked(m_first), jax.ShapeDtypeStruct((n_layers, m_first, n_heads), F32),
                 stacked(m_second), stacked(m_second), jax.ShapeDtypeStruct((n_layers, m_second, n_heads), F32)]
    out_specs = [_rows(tm, att), _rows(tm, att), kv4(first), kv4(first), lf3(first3),
                 kv4(second), kv4(second), lf3(second3)]
    in_specs = [_rows(tm, d), _resident((1, d)), _resident((d, n)), _resident((1, hd)), _resident((1, LANES))]
    args = [x, g.reshape(1, d), w, head_gain.reshape(1, hd), jnp.pad(b_f, (0, LANES - n_heads)).reshape(1, LANES)]
    aliases = {}
    if prev is not None:
        in_specs += [pl.BlockSpec(memory_space=pl.ANY)] * 6
        args += list(prev)
        aliases = {5 + t: 2 + t for t in range(6)}
    kern = functools.partial(_kv_proj_kernel, n_heads=n_heads, head_dim=hd, n_first=n_first,
                             aliased=prev is not None)
    return pl.pallas_call(
        kern, name="kv_proj",
        grid=(m // tm,),
        in_specs=in_specs,
        out_specs=out_specs,
        out_shape=out_shape,
        scratch_shapes=[pltpu.VMEM((tm, d), BF16)],
        input_output_aliases=aliases,
        compiler_params=_cparams("arbitrary"),
    )(*args)


def _ple_kernel(x_ref, g_ref, w_ref, pa_ref, pb_ref, wp_ref, o_ref, h_ref, p_ref, *, n_first):
    h_ref[...] = _rms_rows(x_ref[...], g_ref[...]).astype(BF16)
    p_ref[...] = jnp.where(pl.program_id(0) < n_first, pa_ref[...], pb_ref[...]).astype(BF16)
    for c0 in range(0, o_ref.shape[1], COL_CHUNK):
        cs = slice(c0, c0 + COL_CHUNK)
        gate = jax.nn.sigmoid(jnp.dot(h_ref[...], w_ref[:, cs], preferred_element_type=F32))
        proj = jnp.dot(p_ref[...], wp_ref[:, cs], preferred_element_type=F32)
        o_ref[:, cs] = x_ref[:, cs] + gate * proj


def ple_update(x, g, w, p_first, p_second, wp, tm):
    m, d = x.shape
    pd = wp.shape[0]
    n_first = p_first.shape[0] // tm
    return pl.pallas_call(
        functools.partial(_ple_kernel, n_first=n_first), name="ple_update",
        grid=(m // tm,),
        in_specs=[_rows(tm, d), _resident((1, d)), _resident((d, d)), *_split_rows(tm, pd, n_first),
                  _resident((pd, d))],
        out_specs=_rows(tm, d),
        out_shape=jax.ShapeDtypeStruct((m, d), F32),
        scratch_shapes=[pltpu.VMEM((tm, d), BF16), pltpu.VMEM((tm, pd), BF16)],
        compiler_params=_cparams("parallel"),
    )(x, g.reshape(1, d), w, p_first, p_second, wp)


def _out_proj_kernel(aa_ref, ab_ref, w_ref, x_ref, o_ref, a_ref, *, n_first):
    a_ref[...] = jnp.where(pl.program_id(0) < n_first, aa_ref[...], ab_ref[...])
    for c0 in range(0, o_ref.shape[1], COL_CHUNK):
        cs = slice(c0, c0 + COL_CHUNK)
        o_ref[:, cs] = x_ref[:, cs] + jnp.dot(a_ref[...], w_ref[:, cs], preferred_element_type=F32)


def out_proj_residual(a_first, a_second, w, x, tm):
    m, d = x.shape
    k = w.shape[0]
    n_first = a_first.shape[0] // tm
    return pl.pallas_call(
        functools.partial(_out_proj_kernel, n_first=n_first), name="out_proj_residual",
        grid=(m // tm,),
        in_specs=[*_split_rows(tm, k, n_first), _resident((k, d)), _rows(tm, d)],
        out_specs=_rows(tm, d),
        out_shape=jax.ShapeDtypeStruct((m, d), F32),
        scratch_shapes=[pltpu.VMEM((tm, k), BF16)],
        compiler_params=_cparams("parallel"),
    )(a_first, a_second, w, x)


POOL_HIST_ROWS = 16
CONV_HIST_ROWS = 32


def _cp_mixer_kernel(u_ref, hp_ref, hc_ref, pw_ref, ps_ref, cw_ref, cb_ref, lg_ref, lb_ref,
                     mix_ref, npool_ref, nconv_ref, extp_ref, extc_ref, conv_ref, shift_ref,
                     *, tt, width, start_pos, conv_k, row_chunk, col_chunk):
    ti = pl.program_id(1)
    n_t = pl.num_programs(1)
    ph, ch = POOL_HIST_ROWS, CONV_HIST_ROWS
    w = width
    gd = w // len(POOL_WINDOWS)

    @pl.when(ti == 0)
    def _():
        extp_ref[0:ph, :] = hp_ref[0]
        extc_ref[0:ch, :] = hc_ref[0]

    @pl.when(ti > 0)
    def _():
        extp_ref[0:ph, :] = extp_ref[tt:tt + ph, :]
        extc_ref[0:ch, :] = extc_ref[tt:tt + ch, :]

    extp_ref[ph:ph + tt, :] = u_ref[:, 0:w].astype(F32)
    ua = u_ref[:, w:2 * w].astype(F32)
    ub = u_ref[:, 2 * w:3 * w].astype(F32)
    extc_ref[ch:ch + tt, :] = ua * jax.nn.sigmoid(ub)

    for r0 in range(0, tt, row_chunk):
        pos = (start_pos + 1 + r0 + ti * tt
               + lax.broadcasted_iota(jnp.int32, (row_chunk, 1), 0)).astype(F32)
        for g, win in enumerate(POOL_WINDOWS):
            c0 = g * gd
            cur = extp_ref[ph + r0:ph + r0 + row_chunk, c0:c0 + gd]
            acc = cur
            for j in range(1, win):
                acc = acc + extp_ref[ph + r0 - j:ph + r0 - j + row_chunk, c0:c0 + gd]
            pooled = acc / jnp.minimum(pos, float(win)) - cur
            po = jnp.dot(pooled.astype(BF16), pw_ref[g], preferred_element_type=F32)
            mix_ref[r0:r0 + row_chunk, c0:c0 + gd] = (po * ps_ref[:, c0:c0 + gd]).astype(BF16)

    base = ch - (conv_k - 1)
    n_sh = shift_ref.shape[1]
    for s in range(1, SUBLANES):
        shift_ref[s - 1] = extc_ref[s:s + n_sh, :]

    def tap_rows(k, r0, c0):
        off = base + k
        s = off % SUBLANES
        a = off - s + r0
        if s == 0:
            return extc_ref[a:a + row_chunk, c0:c0 + col_chunk]
        return shift_ref[s - 1, a:a + row_chunk, c0:c0 + col_chunk]

    for r0 in range(0, tt, row_chunk):
        for c0 in range(0, w, col_chunk):
            acc = jnp.broadcast_to(cb_ref[:, c0:c0 + col_chunk], (row_chunk, col_chunk))
            for k in range(conv_k):
                acc = acc + cw_ref[k:k + 1, c0:c0 + col_chunk] * tap_rows(k, r0, c0)
            conv_ref[r0:r0 + row_chunk, c0:c0 + col_chunk] = acc

    for r0 in range(0, tt, row_chunk):
        c = conv_ref[r0:r0 + row_chunk, :]
        mu = jnp.mean(c, axis=-1, keepdims=True)
        cc = c - mu
        var = jnp.mean(cc * cc, axis=-1, keepdims=True)
        y = cc * lax.rsqrt(var + NORM_EPS) * lg_ref[...] + lb_ref[...]
        mix_ref[r0:r0 + row_chunk, w:2 * w] = (y * jax.nn.sigmoid(y)).astype(BF16)

    @pl.when(ti == n_t - 1)
    def _():
        npool_ref[0] = extp_ref[ph + tt - (ph - 1):ph + tt, :]
        nconv_ref[0] = extc_ref[ch + tt - (conv_k - 1):ch + tt, :]


def cp_mixer(u, row0, b, t, hist_pool, hist_conv, pool_w, pool_scale, conv_w, conv_b, ln_g, ln_b, start_pos):
    w = u.shape[1] // 3
    conv_k = conv_w.shape[0]
    n_pool_hist = hist_pool.shape[1]
    assert n_pool_hist == POOL_HIST_ROWS - 1 and conv_k - 1 <= CONV_HIST_ROWS
    tt = _pick(t, (256, 128, 64, 32, 16))
    assert row0 % tt == 0
    n_t = t // tt
    blk0 = row0 // tt
    hp = jnp.pad(hist_pool, ((0, 0), (POOL_HIST_ROWS - n_pool_hist, 0), (0, 0)))
    hc = jnp.pad(hist_conv, ((0, 0), (CONV_HIST_ROWS - (conv_k - 1), 0), (0, 0)))
    kern = functools.partial(_cp_mixer_kernel, tt=tt, width=w, start_pos=start_pos, conv_k=conv_k,
                             row_chunk=min(tt, 64), col_chunk=2 * LANES)
    n_g, gd = pool_w.shape[0], pool_w.shape[1]
    vec = lambda: pl.BlockSpec((1, w), lambda i, j: (0, 0))
    return pl.pallas_call(
        kern, name="cp_mixer",
        grid=(b, n_t),
        in_specs=[
            pl.BlockSpec((tt, 3 * w), lambda i, j: (blk0 + i * n_t + j, 0)),
            pl.BlockSpec((1, POOL_HIST_ROWS, w), lambda i, j: (i, 0, 0)),
            pl.BlockSpec((1, CONV_HIST_ROWS, w), lambda i, j: (i, 0, 0)),
            pl.BlockSpec((n_g, gd, gd), lambda i, j: (0, 0, 0)),
            vec(),
            pl.BlockSpec((conv_k, w), lambda i, j: (0, 0)),
            vec(), vec(), vec(),
        ],
        out_specs=[
            pl.BlockSpec((tt, 2 * w), lambda i, j: (i * n_t + j, 0)),
            pl.BlockSpec((1, n_pool_hist, w), lambda i, j: (i, 0, 0)),
            pl.BlockSpec((1, conv_k - 1, w), lambda i, j: (i, 0, 0)),
        ],
        out_shape=[
            jax.ShapeDtypeStruct((b * t, 2 * w), BF16),
            jax.ShapeDtypeStruct((b, n_pool_hist, w), F32),
            jax.ShapeDtypeStruct((b, conv_k - 1, w), F32),
        ],
        scratch_shapes=[
            pltpu.VMEM((POOL_HIST_ROWS + tt, w), F32),
            pltpu.VMEM((CONV_HIST_ROWS + tt, w), F32),
            pltpu.VMEM((tt, w), F32),
            pltpu.VMEM((SUBLANES - 1, CONV_HIST_ROWS + tt - SUBLANES, w), F32),
        ],
        compiler_params=_cparams("arbitrary", "arbitrary"),
    )(u, hp, hc, pool_w.astype(BF16), pool_scale.reshape(1, w), conv_w, conv_b.reshape(1, w),
      ln_g.reshape(1, w), ln_b.reshape(1, w))


FLASH_HEADS_PER_STEP = 2
FLASH_BLOCK = 512


def _flash_kernel(q_ref, k_ref, v_ref, lf_ref, o_ref, vt_ref, frow_ref, fcol_ref, *, t, n_t, head_dim, n_hp):
    s_len = t * n_t
    hd = head_dim
    n_c = s_len // LANES
    upper = (lax.broadcasted_iota(jnp.int32, (LANES, LANES), 0)
             <= lax.broadcasted_iota(jnp.int32, (LANES, LANES), 1)).astype(F32)
    earlier = (lax.broadcasted_iota(jnp.int32, (n_c, n_c), 1)
               < lax.broadcasted_iota(jnp.int32, (n_c, n_c), 0)).astype(F32)
    for hh in range(n_hp):
        for c in range(0, s_len, t):
            vt_ref[hh, :, c:c + t] = jnp.transpose(v_ref[c:c + t, hh * hd:(hh + 1) * hd].astype(F32)).astype(BF16)
        within = jnp.dot(lf_ref[0, hh], upper, preferred_element_type=F32, precision=lax.Precision.HIGHEST)
        totals = jnp.broadcast_to(within[:, LANES - 1:LANES], (n_c, LANES))
        f_all = (within + jnp.dot(earlier, totals, preferred_element_type=F32,
                                  precision=lax.Precision.HIGHEST)) * LOG2E
        frow_ref[hh] = f_all
        for c in range(n_c):
            fcol_ref[hh, c * LANES:(c + 1) * LANES, :] = jnp.transpose(
                jnp.broadcast_to(f_all[c:c + 1, :], (LANES, LANES)))

    below_diag = (lax.broadcasted_iota(jnp.int32, (t, t), 0) <= lax.broadcasted_iota(jnp.int32, (t, t), 1))

    def q_block(i, _):
        q0 = pl.multiple_of(i * t, t)
        qs = [q_ref[pl.ds(q0, t), hh * hd:(hh + 1) * hd] for hh in range(n_hp)]
        c0 = i * (t // LANES)
        fqs = [jnp.concatenate([frow_ref[hh, pl.ds(c0 + r, 1), :] for r in range(t // LANES)], axis=1)
               for hh in range(n_hp)]

        def step(j, carry, masked):
            k0 = pl.multiple_of(j * t, t)
            out = []
            for hh in range(n_hp):
                m, l, acc = carry[hh]
                k = k_ref[pl.ds(k0, t), hh * hd:(hh + 1) * hd]
                f_k = fcol_ref[hh, pl.ds(k0, t), :]
                s = lax.dot_general(k, qs[hh], _NT, preferred_element_type=F32)
                s = s + (fqs[hh] - jnp.tile(f_k, (1, t // LANES)))
                if masked:
                    s = jnp.where(below_diag, s, NEG_INF)
                m_new = jnp.maximum(m, jnp.max(s, axis=0, keepdims=True))
                alpha = jnp.exp2(m - m_new)
                p = jnp.exp2(s - m_new)
                l_new = alpha * l + jnp.sum(p, axis=0, keepdims=True)
                pv = jnp.dot(vt_ref[hh, :, pl.ds(k0, t)], p.astype(BF16), preferred_element_type=F32)
                out.append((m_new, l_new, alpha * acc + pv))
            return tuple(out)

        init = tuple((jnp.full((1, t), NEG_INF, F32), jnp.zeros((1, t), F32), jnp.zeros((hd, t), F32))
                     for _ in range(n_hp))
        carry = lax.fori_loop(0, i, lambda j, c: step(j, c, False), init)
        carry = step(i, carry, True)
        for hh in range(n_hp):
            _, l, acc = carry[hh]
            o_ref[pl.ds(q0, t), hh * hd:(hh + 1) * hd] = jnp.transpose(acc / l).astype(o_ref.dtype)
        return 0

    lax.fori_loop(0, n_t, q_block, 0)


def flash_prompt(q, k, v, logf, b, s_len, *, n_heads, head_dim):
    n_hp = FLASH_HEADS_PER_STEP
    t = _pick(s_len, (FLASH_BLOCK, 256, 128))
    n_c = s_len // LANES
    kern = functools.partial(_flash_kernel, t=t, n_t=s_len // t, head_dim=head_dim, n_hp=n_hp)
    heads = lambda: pl.BlockSpec((s_len, n_hp * head_dim), lambda i, h: (i, h))
    return pl.pallas_call(
        kern, name="flash_prompt",
        grid=(b, n_heads // n_hp),
        in_specs=[heads(), heads(), heads(), pl.BlockSpec((1, n_hp, n_c, LANES), lambda i, h: (i, h, 0, 0))],
        out_specs=heads(),
        out_shape=jax.ShapeDtypeStruct((b * s_len, n_heads * head_dim), BF16),
        scratch_shapes=[pltpu.VMEM((n_hp, head_dim, s_len), BF16), pltpu.VMEM((n_hp, n_c, LANES), F32),
                        pltpu.VMEM((n_hp, s_len, LANES), F32)],
        compiler_params=_cparams("parallel", "parallel"),
    )(q, k, v, logf)


def _sample_attn_kernel(q_ref, kn_ref, vn_ref, kc_ref, vc_ref, lf_ref, o_ref,
                        qbd_ref, m_ref, l_ref, acc_ref, kmat_ref, vmat_ref, fs_ref, *, t, n_heads, head_dim, past):
    j = pl.program_id(1)
    tp = kmat_ref.shape[0]

    @pl.when(j == 0)
    def _():
        upper = (lax.broadcasted_iota(jnp.int32, (LANES, LANES), 0)
                 <= lax.broadcasted_iota(jnp.int32, (LANES, LANES), 1)).astype(F32)
        carry = jnp.zeros((n_heads, 1), F32)
        for c0 in range(0, fs_ref.shape[1], LANES):
            f_c = carry + jnp.dot(lf_ref[0, :, c0:c0 + LANES], upper, preferred_element_type=F32,
                                  precision=lax.Precision.HIGHEST)
            fs_ref[:, c0:c0 + LANES] = f_c
            carry = f_c[:, LANES - 1:LANES]

    rows = 64

    def relayout(c, _):
        r0 = pl.multiple_of(c * rows, rows)
        for h in range(n_heads):
            kmat_ref[pl.ds(r0, rows), h * head_dim:(h + 1) * head_dim] = kc_ref[0, 0, pl.ds(r0, rows), h, :]
            vmat_ref[pl.ds(r0, rows), h * head_dim:(h + 1) * head_dim] = vc_ref[0, 0, pl.ds(r0, rows), h, :]
        return 0

    lax.fori_loop(0, kmat_ref.shape[0] // rows, relayout, 0)
    n_j = pl.num_programs(1)
    r = n_heads * t
    hd = n_heads * head_dim

    def same_head():
        return (lax.broadcasted_iota(jnp.int32, (r, hd), 0) // t
                == lax.broadcasted_iota(jnp.int32, (r, hd), 1) // head_dim)

    def rep_rows(x):
        return jnp.broadcast_to(x[:, None, :], (n_heads, t, x.shape[-1])).reshape(r, x.shape[-1])

    f_new = rep_rows(fs_ref[:, past:past + t]) * LOG2E
    row_t = lax.broadcasted_iota(jnp.int32, (r, t), 0) % t
    col_t = lax.broadcasted_iota(jnp.int32, (r, t), 1)
    f_q = jnp.sum(jnp.where(col_t == row_t, f_new, 0.0), axis=-1, keepdims=True)

    @pl.when(j == 0)
    def _():
        q_rep = jnp.broadcast_to(q_ref[...][None], (n_heads, t, hd)).reshape(r, hd)
        qbd_ref[...] = jnp.where(same_head(), q_rep, jnp.zeros_like(q_rep))
        m_ref[...] = jnp.full(m_ref.shape, NEG_INF, F32)
        l_ref[...] = jnp.zeros(l_ref.shape, F32)
        acc_ref[...] = jnp.zeros(acc_ref.shape, F32)

    def update(k, v, bias, valid):
        s = lax.dot_general(qbd_ref[...], k, _NT, preferred_element_type=F32) + bias
        if valid is not None:
            s = jnp.where(valid, s, NEG_INF)
        m = m_ref[...]
        m_new = jnp.maximum(m, jnp.max(s, axis=-1, keepdims=True))
        alpha = jnp.exp2(m - m_new)
        p = jnp.exp2(s - m_new)
        l_ref[...] = alpha * l_ref[...] + jnp.sum(p, axis=-1, keepdims=True)
        acc_ref[...] = alpha * acc_ref[...] + jnp.dot(p.astype(BF16), v, preferred_element_type=F32)
        m_ref[...] = m_new

    f_past = fs_ref[:, pl.ds(pl.multiple_of(j * tp, tp), tp)]
    update(kmat_ref[...].astype(BF16), vmat_ref[...].astype(BF16), f_q - rep_rows(f_past) * LOG2E, None)

    @pl.when(j == n_j - 1)
    def _():
        update(kn_ref[...], vn_ref[...], f_q - f_new, col_t <= row_t)
        o_full = jnp.where(same_head(), acc_ref[...] / l_ref[...], 0.0)
        o = o_full[0:t]
        for h in range(1, n_heads):
            o = o + o_full[h * t:(h + 1) * t]
        o_ref[...] = o.astype(o_ref.dtype)


def sample_attention(q, k, v, row0, b, t, cache_k, cache_v, layer, logf, *, n_heads, head_dim):
    hd = n_heads * head_dim
    p = cache_k.shape[2]
    tp = _pick(p, (512, 256, 128))
    r = n_heads * t
    assert row0 % t == 0
    blk0 = row0 // t
    n_pos = -(-(p + t) // LANES) * LANES
    logf = jnp.pad(logf, ((0, 0), (0, 0), (0, n_pos - (p + t))))
    kern = functools.partial(_sample_attn_kernel, t=t, n_heads=n_heads, head_dim=head_dim, past=p)
    new = lambda: pl.BlockSpec((t, hd), lambda i, j: (blk0 + i, 0))
    past = lambda: pl.BlockSpec((1, 1, tp, n_heads, head_dim), lambda i, j: (layer, i, j, 0, 0))
    return pl.pallas_call(
        kern, name="sample_attention",
        grid=(b, p // tp),
        in_specs=[new(), new(), new(), past(), past(),
                  pl.BlockSpec((1, n_heads, n_pos), lambda i, j: (i, 0, 0))],
        out_specs=pl.BlockSpec((t, hd), lambda i, j: (i, 0)),
        out_shape=jax.ShapeDtypeStruct((b * t, hd), BF16),
        scratch_shapes=[pltpu.VMEM((r, hd), BF16), pltpu.VMEM((r, 1), F32), pltpu.VMEM((r, 1), F32),
                        pltpu.VMEM((r, hd), F32), pltpu.VMEM((tp, hd), F32), pltpu.VMEM((tp, hd), F32),
                        pltpu.VMEM((n_heads, n_pos), F32)],
        compiler_params=_cparams("parallel", "arbitrary"),
    )(q, k, v, cache_k, cache_v, logf)


def _router_kernel(x_ref, g_ref, w_ref, b_ref, ids_ref, wts_ref, cnt_ref, seen_ref, whi_ref, wlo_ref,
                   *, n_groups, per_group):
    @pl.when(pl.program_id(0) == 0)
    def _():
        seen_ref[...] = jnp.zeros(seen_ref.shape, F32)
        w_hi = w_ref[...].astype(BF16)
        whi_ref[...] = w_hi
        wlo_ref[...] = (w_ref[...] - w_hi.astype(F32)).astype(BF16)

    h = _rms_rows(x_ref[...], g_ref[...])
    h_hi = h.astype(BF16)
    h_lo = (h - h_hi.astype(F32)).astype(BF16)
    logits = (jnp.dot(h_hi, whi_ref[...], preferred_element_type=F32)
              + jnp.dot(h_lo, whi_ref[...], preferred_element_type=F32)
              + jnp.dot(h_hi, wlo_ref[...], preferred_element_type=F32)) + b_ref[...]
    tm, n = logits.shape
    col = lax.broadcasted_iota(jnp.int32, (tm, n), 1)
    neg = jnp.float32(-jnp.inf)

    def first_max(vals):
        top = jnp.max(vals, axis=-1, keepdims=True)
        idx = jnp.min(jnp.where(vals == top, col, n), axis=-1, keepdims=True)
        return top, idx

    gl = jnp.where(col < n_groups, logits, neg)
    g_top, g_sel = first_max(gl)
    g_prob = 1.0 / jnp.sum(jnp.exp(gl - g_top), axis=-1, keepdims=True)
    lo = n_groups + g_sel * per_group
    el = jnp.where((col >= lo) & (col < lo + per_group), logits, neg)
    v1, i1 = first_max(el)
    v2, i2 = first_max(jnp.where(col == i1, neg, el))
    e2 = jnp.exp(v2 - v1)
    w1 = g_prob / (1.0 + e2)
    w2 = g_prob * e2 / (1.0 + e2)
    e1 = i1 - n_groups
    e2 = i2 - n_groups
    chosen = jnp.where((col == e1) | (col == e2), 1.0, 0.0)
    earlier_rows = (lax.broadcasted_iota(jnp.int32, (tm, tm), 1)
                    < lax.broadcasted_iota(jnp.int32, (tm, tm), 0))
    before = seen_ref[...] + jnp.dot(jnp.where(earlier_rows, 1.0, 0.0).astype(BF16), chosen.astype(BF16),
                                     preferred_element_type=F32)
    r1 = jnp.sum(jnp.where(col == e1, before, 0.0), axis=-1, keepdims=True).astype(jnp.int32)
    r2 = jnp.sum(jnp.where(col == e2, before, 0.0), axis=-1, keepdims=True).astype(jnp.int32)
    seen_ref[...] = seen_ref[...] + jnp.sum(chosen, axis=0, keepdims=True)
    cnt_ref[...] = seen_ref[...].astype(jnp.int32)
    ids_ref[...] = jnp.where(col == 0, e1, jnp.where(col == 1, e2, jnp.where(col == 2, r1,
                                                                             jnp.where(col == 3, r2, 0))))
    wts_ref[...] = jnp.where(col == 0, w1, jnp.where(col == 1, w2, 0.0))


def router(x, g, w_pad, b_pad, *, n_groups, per_group):
    m, d = x.shape
    n = w_pad.shape[1]
    tm = _pick(m, (256, 128, 64, 32, 16, 8))
    kern = functools.partial(_router_kernel, n_groups=n_groups, per_group=per_group)
    return pl.pallas_call(
        kern, name="router",
        grid=(m // tm,),
        in_specs=[_rows(tm, d), _resident((1, d)), _resident((d, n)), _resident((1, n))],
        out_specs=[_rows(tm, n), _rows(tm, n), _resident((1, n))],
        out_shape=[jax.ShapeDtypeStruct((m, n), jnp.int32), jax.ShapeDtypeStruct((m, n), F32),
                   jax.ShapeDtypeStruct((1, n), jnp.int32)],
        scratch_shapes=[pltpu.VMEM((1, n), F32), pltpu.VMEM((d, n), BF16), pltpu.VMEM((d, n), BF16)],
        compiler_params=_cparams("arbitrary"),
    )(x, g.reshape(1, d), w_pad, b_pad.reshape(1, n))


EXPERT_BLOCK_ROWS = 256


def _row_copy(src_ref, src_row, dst_ref, dst_row, sem):
    return pltpu.make_async_copy(src_ref.at[pl.ds(src_row, 1)], dst_ref.at[pl.ds(dst_row, 1)], sem)


SEL_FIELDS = 2 * TOP_K
DMA_UNROLL = 8


def _dest_row(start_ref, sel_ref, r, k):
    return start_ref[sel_ref[0, 0, SEL_FIELDS * r + k]] + sel_ref[0, 0, SEL_FIELDS * r + TOP_K + k]


def _sel_blocks(ids, tm):
    m = ids.shape[0]
    return ids[:, :SEL_FIELDS].reshape(m // tm, 1, tm * SEL_FIELDS)


def _dispatch_kernel(start_ref, sel_ref, x_ref, buf_in_ref, buf_ref, sem, *, tm):
    del buf_in_ref

    def copies(r):
        return [_row_copy(x_ref, r, buf_ref, _dest_row(start_ref, sel_ref, r, k), sem) for k in range(TOP_K)]

    def start(r, _):
        for c in copies(r):
            c.start()
        return 0

    def wait(r, _):
        for c in copies(r):
            c.wait()
        return 0

    lax.fori_loop(0, tm, start, 0, unroll=DMA_UNROLL)
    lax.fori_loop(0, tm, wait, 0, unroll=DMA_UNROLL)


def dispatch_rows(x, ids, pad_start, buf):
    m, d = x.shape
    tm = _pick(m, (256, 128, 64, 32, 16, 8))
    return pl.pallas_call(
        functools.partial(_dispatch_kernel, tm=tm), name="moe_dispatch",
        grid_spec=pltpu.PrefetchScalarGridSpec(
            num_scalar_prefetch=1,
            grid=(m // tm,),
            in_specs=[pl.BlockSpec((1, 1, tm * SEL_FIELDS), lambda i, st: (i, 0, 0), memory_space=pltpu.SMEM),
                      pl.BlockSpec((tm, d), lambda i, st: (i, 0)),
                      pl.BlockSpec(memory_space=pl.ANY)],
            out_specs=pl.BlockSpec(memory_space=pl.ANY),
            scratch_shapes=[pltpu.SemaphoreType.DMA(())],
        ),
        out_shape=jax.ShapeDtypeStruct(buf.shape, buf.dtype),
        input_output_aliases={3: 0},
        compiler_params=_cparams("arbitrary"),
    )(pad_start, _sel_blocks(ids, tm), x, buf)


def _expert_kernel(be_ref, nu_ref, x_ref, g_ref, wg_ref, wu_ref, wd_ref, o_ref, wg_b, wu_b, wd_b):
    i = pl.program_id(0)
    used = i < nu_ref[0]

    @pl.when(used & ((i == 0) | (be_ref[i] != be_ref[jnp.maximum(i - 1, 0)])))
    def _():
        wg_b[...] = wg_ref[0, 0].astype(BF16)
        wu_b[...] = wu_ref[0, 0].astype(BF16)
        wd_b[...] = wd_ref[0, 0].astype(BF16)

    @pl.when(used)
    def _():
        h = _rms_rows(x_ref[...], g_ref[...]).astype(BF16)
        gate = jnp.dot(h, wg_b[...], preferred_element_type=F32)
        up = jnp.dot(h, wu_b[...], preferred_element_type=F32)
        act = (gate * jax.nn.sigmoid(gate) * up).astype(BF16)
        o_ref[...] = jnp.dot(act, wd_b[...], preferred_element_type=F32)

    @pl.when(jnp.logical_not(used))
    def _():
        o_ref[...] = jnp.zeros(o_ref.shape, o_ref.dtype)


def expert_mlps(buf, g, w_gate, w_up, w_down, layer, block_expert, n_used):
    rows, d = buf.shape
    de = w_gate.shape[3]
    br = EXPERT_BLOCK_ROWS
    nb = rows // br
    blk = lambda i, be, nu: (jnp.minimum(i, nu[0] - 1), 0)
    wsel = lambda i, be, nu: (layer, be[i], 0, 0)
    return pl.pallas_call(
        _expert_kernel, name="expert_mlps",
        grid_spec=pltpu.PrefetchScalarGridSpec(
            num_scalar_prefetch=2,
            grid=(nb,),
            in_specs=[pl.BlockSpec((br, d), blk),
                      pl.BlockSpec((1, d), lambda i, be, nu: (0, 0)),
                      pl.BlockSpec((1, 1, d, de), wsel),
                      pl.BlockSpec((1, 1, d, de), wsel),
                      pl.BlockSpec((1, 1, de, d), wsel)],
            out_specs=pl.BlockSpec((br, d), lambda i, be, nu: (i, 0)),
            scratch_shapes=[pltpu.VMEM((d, de), BF16), pltpu.VMEM((d, de), BF16), pltpu.VMEM((de, d), BF16)],
        ),
        out_shape=jax.ShapeDtypeStruct((rows, d), F32),
        compiler_params=_cparams("arbitrary"),
    )(block_expert, n_used, buf, g.reshape(1, d), w_gate, w_up, w_down)


def _combine_kernel(start_ref, sel_cur_ref, sel_nxt_ref, x_ref, wts_ref, y_ref, o_ref, rows_ref, sems, *, tm):
    i = pl.program_id(0)
    n = pl.num_programs(0)

    def copies(sel_ref, slot, r):
        return [_row_copy(y_ref, _dest_row(start_ref, sel_ref, r, k), rows_ref.at[slot, k], r, sems.at[slot])
                for k in range(TOP_K)]

    def start_all(sel_ref, slot):
        def body(r, _):
            for c in copies(sel_ref, slot, r):
                c.start()
            return 0
        lax.fori_loop(0, tm, body, 0, unroll=DMA_UNROLL)

    @pl.when(i == 0)
    def _():
        start_all(sel_cur_ref, 0)

    @pl.when(i + 1 < n)
    def _():
        start_all(sel_nxt_ref, (i + 1) % 2)

    slot = i % 2

    def wait_body(r, _):
        for c in copies(sel_cur_ref, slot, r):
            c.wait()
        return 0
    lax.fori_loop(0, tm, wait_body, 0, unroll=DMA_UNROLL)

    wts = wts_ref[...]
    o_ref[...] = (x_ref[...] + wts[:, 0:1] * rows_ref[slot, 0] + wts[:, 1:2] * rows_ref[slot, 1])


def combine_rows(x, wts, y, ids, pad_start):
    m, d = x.shape
    tm = _pick(m, (256, 128, 64, 32, 16, 8))
    n_t = m // tm
    sel = _sel_blocks(ids, tm)
    smem = lambda f: pl.BlockSpec((1, 1, tm * SEL_FIELDS), f, memory_space=pltpu.SMEM)
    return pl.pallas_call(
        functools.partial(_combine_kernel, tm=tm), name="moe_combine",
        grid_spec=pltpu.PrefetchScalarGridSpec(
            num_scalar_prefetch=1,
            grid=(n_t,),
            in_specs=[smem(lambda i, st: (i, 0, 0)),
                      smem(lambda i, st: (jnp.minimum(i + 1, n_t - 1), 0, 0)),
                      pl.BlockSpec((tm, d), lambda i, st: (i, 0)),
                      pl.BlockSpec((tm, wts.shape[1]), lambda i, st: (i, 0)),
                      pl.BlockSpec(memory_space=pl.ANY)],
            out_specs=pl.BlockSpec((tm, d), lambda i, st: (i, 0)),
            scratch_shapes=[pltpu.VMEM((2, TOP_K, tm, d), F32), pltpu.SemaphoreType.DMA((2,))],
        ),
        out_shape=jax.ShapeDtypeStruct((m, d), F32),
        compiler_params=_cparams("arbitrary"),
    )(pad_start, sel, sel, x, wts, y)


def _block_tables(counts, block_rows, n_blocks):
    n_experts = counts.shape[0]
    padded = (counts + block_rows - 1) // block_rows * block_rows
    pad_end = jnp.cumsum(padded)
    pad_start = (pad_end - padded).astype(jnp.int32)
    block_first_row = jnp.arange(n_blocks, dtype=jnp.int32) * block_rows
    block_expert = jnp.minimum(jnp.sum(pad_end[None, :] <= block_first_row[:, None], axis=1),
                               n_experts - 1).astype(jnp.int32)
    n_used = (pad_end[-1:] // block_rows).astype(jnp.int32)
    return pad_start, block_expert, n_used


def moe_buffer_rows(m, n_experts):
    return (-(-m * TOP_K // EXPERT_BLOCK_ROWS) + n_experts) * EXPERT_BLOCK_ROWS


def hier_moe(x, buf, g, w_router, b_router, w_gate, w_up, w_down, layer, *, n_groups, per_group):
    n_experts = n_groups * per_group
    ids, wts, counts = router(x, g, w_router, b_router, n_groups=n_groups, per_group=per_group)
    pad_start, block_expert, n_used = _block_tables(counts[0, :n_experts], EXPERT_BLOCK_ROWS,
                                                    buf.shape[0] // EXPERT_BLOCK_ROWS)
    buf = dispatch_rows(x, ids, pad_start, buf)
    y = expert_mlps(buf, g, w_gate, w_up, w_down, layer, block_expert, n_used)
    return combine_rows(x, wts, y, ids, pad_start), buf


def _pad_cols(w, n):
    return jnp.pad(w, ((0, 0), (0, n - w.shape[1])))


def kernel(x_prompt, x_sample, cache_k, cache_v, cache_logf, state_pool, state_conv, p_prompt, p_sample,
           norm_mix, norm_ffn, norm_ple, w_in_cp, pool_w, pool_scale, conv_w, conv_b, conv_ln_g, conv_ln_b,
           w_out_cp, w_in_fox, b_forget, q_norm, k_norm, w_out_fox, router_group_w, router_group_b,
           router_expert_w, router_expert_b, expert_w_gate, expert_w_up, expert_w_down, w_ple_gate, w_ple_proj):
    bp, sp, d = x_prompt.shape
    bs, ts, _ = x_sample.shape
    depth = norm_mix.shape[0]
    n_fox = w_in_fox.shape[0]
    n_heads = b_forget.shape[1]
    head_dim = q_norm.shape[1]
    att = n_heads * head_dim
    past = cache_k.shape[2]
    n_groups = router_group_w.shape[2]
    n_experts = router_expert_w.shape[2]
    per_group = n_experts // n_groups
    width = pool_w.shape[1] * pool_w.shape[2]
    q_scale = float(head_dim) ** -0.5 * LOG2E

    mp, ms = bp * sp, bs * ts
    tm = _row_tile(mp, ms)
    x = jnp.concatenate([x_prompt.reshape(mp, d), x_sample.reshape(ms, d)], axis=0)
    pools_p, convs_p, pools_s, convs_s = [], [], [], []
    fox_out = None
    moe_buf = jnp.zeros((moe_buffer_rows(mp + ms, n_experts), d), F32)

    for i in range(depth):
        j = i // 2
        if i % 2 == 0:
            u = cp_in_proj(x, norm_mix[i], w_in_cp[j].astype(BF16), tm)
            mixer = functools.partial(cp_mixer, pool_w=pool_w[j], pool_scale=pool_scale[j], conv_w=conv_w[j],
                                      conv_b=conv_b[j], ln_g=conv_ln_g[j], ln_b=conv_ln_b[j])
            mix_p, n_pool, n_conv = mixer(u, 0, bp, sp, jnp.zeros((bp,) + state_pool.shape[2:], F32),
                                          jnp.zeros((bp,) + state_conv.shape[2:], F32), start_pos=0)
            pools_p.append(n_pool)
            convs_p.append(n_conv)
            mix_s, n_pool, n_conv = mixer(u, mp, bs, ts, state_pool[j], state_conv[j], start_pos=past)
            pools_s.append(n_pool)
            convs_s.append(n_conv)
            x = out_proj_residual(mix_p, mix_s, w_out_cp[j].astype(BF16), x, tm)
        else:
            w_in = w_in_fox[j]
            g = norm_mix[i]
            q = q_proj(x, g, w_in[:, :att].astype(BF16), q_norm[j], q_scale, tm)
            w_kvf = _pad_cols(w_in[:, att:], 2 * att + LANES).astype(BF16)
            res = kv_proj(x, g, w_kvf, k_norm[j], b_forget[j], j, n_fox, mp, fox_out, min(tm, 256))
            k16, v16 = res[0], res[1]
            fox_out = res[2:]
            lf_p = fox_out[2][j].reshape(bp, sp, n_heads)
            lf_s = fox_out[5][j].reshape(bs, ts, n_heads)
            o_p = flash_prompt(q, k16, v16, lf_p.transpose(0, 2, 1).reshape(bp, n_heads, sp // LANES, LANES),
                               bp, sp, n_heads=n_heads, head_dim=head_dim)
            lf_all = jnp.concatenate([cache_logf[j], lf_s], axis=1).transpose(0, 2, 1)
            o_s = sample_attention(q, k16, v16, mp, bs, ts, cache_k, cache_v, j, lf_all,
                                   n_heads=n_heads, head_dim=head_dim)
            x = out_proj_residual(o_p, o_s, w_out_fox[j].astype(BF16), x, tm)

        w_router = _pad_cols(jnp.concatenate([router_group_w[i], router_expert_w[i]], axis=1), LANES)
        b_router = jnp.pad(jnp.concatenate([router_group_b[i], router_expert_b[i]]),
                           (0, LANES - n_groups - n_experts))
        x, moe_buf = hier_moe(x, moe_buf, norm_ffn[i], w_router, b_router, expert_w_gate, expert_w_up,
                              expert_w_down, i, n_groups=n_groups, per_group=per_group)
        x = ple_update(x, norm_ple[i], w_ple_gate[i].astype(BF16), p_prompt[i].reshape(mp, -1),
                       p_sample[i].reshape(ms, -1), w_ple_proj[i].astype(BF16), tm)

    k4p, v4p, lfp, k4s, v4s, lfs = fox_out
    return (x[:mp].reshape(bp, sp, d), x[mp:].reshape(bs, ts, d),
            k4p.reshape(n_fox, bp, sp, n_heads, head_dim), v4p.reshape(n_fox, bp, sp, n_heads, head_dim),
            lfp.reshape(n_fox, bp, sp, n_heads), jnp.stack(pools_p), jnp.stack(convs_p),
            k4s.reshape(n_fox, bs, ts, n_heads, head_dim), v4s.reshape(n_fox, bs, ts, n_heads, head_dim),
            lfs.reshape(n_fox, bs, ts, n_heads), jnp.stack(pools_s), jnp.stack(convs_s))
```

```python
import functools
import math

import jax
import jax.numpy as jnp
from jax import lax
from jax.experimental import pallas as pl
from jax.experimental.pallas import tpu as pltpu

F32 = jnp.float32
BF16 = jnp.bfloat16

NORM_EPS = 1e-6
NEG_INF = -1e30
LOG2E = math.log2(math.e)
POOL_WINDOWS = (2, 4, 8, 16)
TOP_K = 2

V7X_VMEM_BYTES = 64 * 1024 * 1024
VMEM_LIMIT = V7X_VMEM_BYTES - 8 * 1024 * 1024
LANES = 128
SUBLANES = 8

ROW_TILE = 512
COL_CHUNK = 512
_NT = (((1,), (1,)), ((), ()))


def _cparams(*sem):
    return pltpu.CompilerParams(dimension_semantics=sem, vmem_limit_bytes=VMEM_LIMIT)


def _pick(n, prefs):
    for p in prefs:
        if n % p == 0:
            return p
    return n


def _rms_rows(x, g):
    ms = jnp.mean(x * x, axis=-1, keepdims=True)
    return x * lax.rsqrt(ms + NORM_EPS) * g


def _resident(shape):
    zeros = (0,) * len(shape)
    return pl.BlockSpec(shape, lambda i: zeros)


def _rows(tm, width):
    return pl.BlockSpec((tm, width), lambda i: (i, 0))


def _split_rows(tm, width, n_first):
    return (pl.BlockSpec((tm, width), lambda i: (jnp.minimum(i, n_first - 1), 0)),
            pl.BlockSpec((tm, width), lambda i: (jnp.maximum(i - n_first, 0), 0)))


def _row_tile(m_first, m_second):
    return _pick(math.gcd(m_first, m_second), (ROW_TILE, 256, 128, 64, 32, 16))


def _cp_in_kernel(x_ref, g_ref, w_ref, o_ref, h_ref):
    h_ref[...] = _rms_rows(x_ref[...], g_ref[...]).astype(BF16)
    for c0 in range(0, o_ref.shape[1], COL_CHUNK):
        o_ref[:, c0:c0 + COL_CHUNK] = jnp.dot(
            h_ref[...], w_ref[:, c0:c0 + COL_CHUNK], preferred_element_type=F32).astype(o_ref.dtype)


def cp_in_proj(x, g, w, tm):
    m, d = x.shape
    n = w.shape[1]
    return pl.pallas_call(
        _cp_in_kernel, name="cp_in_proj",
        grid=(m // tm,),
        in_specs=[_rows(tm, d), _resident((1, d)), _resident((d, n))],
        out_specs=_rows(tm, n),
        out_shape=jax.ShapeDtypeStruct((m, n), BF16),
        scratch_shapes=[pltpu.VMEM((tm, d), BF16)],
        compiler_params=_cparams("parallel"),
    )(x, g.reshape(1, d), w)


def _q_proj_kernel(x_ref, g_ref, w_ref, hg_ref, o_ref, h_ref, *, head_dim, scale):
    h_ref[...] = _rms_rows(x_ref[...], g_ref[...]).astype(BF16)
    for c0 in range(0, o_ref.shape[1], COL_CHUNK):
        acc = jnp.dot(h_ref[...], w_ref[:, c0:c0 + COL_CHUNK], preferred_element_type=F32)
        for h0 in range(0, COL_CHUNK, head_dim):
            a = _rms_rows(acc[:, h0:h0 + head_dim], hg_ref[...]) * scale
            o_ref[:, c0 + h0:c0 + h0 + head_dim] = a.astype(BF16)


def q_proj(x, g, w, head_gain, scale, tm):
    m, d = x.shape
    n = w.shape[1]
    hd = head_gain.shape[0]
    return pl.pallas_call(
        functools.partial(_q_proj_kernel, head_dim=hd, scale=scale), name="q_proj",
        grid=(m // tm,),
        in_specs=[_rows(tm, d), _resident((1, d)), _resident((d, n)), _resident((1, hd))],
        out_specs=_rows(tm, n),
        out_shape=jax.ShapeDtypeStruct((m, n), BF16),
        scratch_shapes=[pltpu.VMEM((tm, d), BF16)],
        compiler_params=_cparams("parallel"),
    )(x, g.reshape(1, d), w, head_gain.reshape(1, hd))


def _kv_proj_kernel(x_ref, g_ref, w_ref, hg_ref, bf_ref, *rest, n_heads, head_dim, n_first, aliased):
    if aliased:
        rest = rest[6:]
    k_ref, v_ref, k4p_ref, v4p_ref, lfp_ref, k4s_ref, v4s_ref, lfs_ref, h_ref = rest
    att = n_heads * head_dim
    h_ref[...] = _rms_rows(x_ref[...], g_ref[...]).astype(BF16)
    i = pl.program_id(0)

    def emit(k4_ref, v4_ref, lf_ref):
        for c0 in range(0, 2 * att, COL_CHUNK):
            acc = jnp.dot(h_ref[...], w_ref[:, c0:c0 + COL_CHUNK], preferred_element_type=F32)
            for h0 in range(0, COL_CHUNK, head_dim):
                col = c0 + h0
                head = (col % att) // head_dim
                a = acc[:, h0:h0 + head_dim]
                if col < att:
                    a = _rms_rows(a, hg_ref[...])
                    k_ref[:, col:col + head_dim] = a.astype(BF16)
                    k4_ref[0, :, head, :] = a
                else:
                    v_ref[:, col - att:col - att + head_dim] = a.astype(BF16)
                    v4_ref[0, :, head, :] = a
        z = jnp.dot(h_ref[...], w_ref[:, 2 * att:2 * att + LANES], preferred_element_type=F32) + bf_ref[...]
        logf = jnp.minimum(z, 0.0) - jnp.log1p(jnp.exp(-jnp.abs(z)))
        lf_ref[0] = logf[:, 0:n_heads]

    @pl.when(i < n_first)
    def _():
        emit(k4p_ref, v4p_ref, lfp_ref)

    @pl.when(i >= n_first)
    def _():
        emit(k4s_ref, v4s_ref, lfs_ref)


def kv_proj(x, g, w, head_gain, b_f, layer, n_layers, m_first, prev, tm):
    m, d = x.shape
    n = w.shape[1]
    hd = head_gain.shape[0]
    n_heads = b_f.shape[0]
    att = n_heads * hd
    n_first = m_first // tm
    m_second = m - m_first
    first = lambda i: (layer, jnp.minimum(i, n_first - 1), 0, 0)
    second = lambda i: (layer, jnp.maximum(i - n_first, 0), 0, 0)
    first3 = lambda i: (layer, jnp.minimum(i, n_first - 1), 0)
    second3 = lambda i: (layer, jnp.maximum(i - n_first, 0), 0)
    kv4 = lambda f: pl.BlockSpec((1, tm, n_heads, hd), f)
    lf3 = lambda f: pl.BlockSpec((1, tm, n_heads), f)
    stacked = lambda rows: jax.ShapeDtypeStruct((n_layers, rows, n_heads, hd), F32)
    out_shape = [jax.ShapeDtypeStruct((m, att), BF16), jax.ShapeDtypeStruct((m, att), BF16),
                 stacked(m_first), stacked(m_first), jax.ShapeDtypeStruct((n_layers, m_first, n_heads), F32),
                 stacked(m_second), stacked(m_second), jax.ShapeDtypeStruct((n_layers, m_second, n_heads), F32)]
    out_specs = [_rows(tm, att), _rows(tm, att), kv4(first), kv4(first), lf3(first3),
                 kv4(second), kv4(second), lf3(second3)]
    in_specs = [_rows(tm, d), _resident((1, d)), _resident((d, n)), _resident((1, hd)), _resident((1, LANES))]
    args = [x, g.reshape(1, d), w, head_gain.reshape(1, hd), jnp.pad(b_f, (0, LANES - n_heads)).reshape(1, LANES)]
    aliases = {}
    if prev is not None:
        in_specs += [pl.BlockSpec(memory_space=pl.ANY)] * 6
        args += list(prev)
        aliases = {5 + t: 2 + t for t in range(6)}
    kern = functools.partial(_kv_proj_kernel, n_heads=n_heads, head_dim=hd, n_first=n_first,
                             aliased=prev is not None)
    return pl.pallas_call(
        kern, name="kv_proj",
        grid=(m // tm,),
        in_specs=in_specs,
        out_specs=out_specs,
        out_shape=out_shape,
        scratch_shapes=[pltpu.VMEM((tm, d), BF16)],
        input_output_aliases=aliases,
        compiler_params=_cparams("arbitrary"),
    )(*args)


def _out_proj_kernel(aa_ref, ab_ref, w_ref, x_ref, o_ref, a_ref, *, n_first):
    a_ref[...] = jnp.where(pl.program_id(0) < n_first, aa_ref[...], ab_ref[...])
    for c0 in range(0, o_ref.shape[1], COL_CHUNK):
        cs = slice(c0, c0 + COL_CHUNK)
        o_ref[:, cs] = x_ref[:, cs] + jnp.dot(a_ref[...], w_ref[:, cs], preferred_element_type=F32)


def out_proj_residual(a_first, a_second, w, x, tm):
    m, d = x.shape
    k = w.shape[0]
    n_first = a_first.shape[0] // tm
    return pl.pallas_call(
        functools.partial(_out_proj_kernel, n_first=n_first), name="out_proj_residual",
        grid=(m // tm,),
        in_specs=[*_split_rows(tm, k, n_first), _resident((k, d)), _rows(tm, d)],
        out_specs=_rows(tm, d),
        out_shape=jax.ShapeDtypeStruct((m, d), F32),
        scratch_shapes=[pltpu.VMEM((tm, k), BF16)],
        compiler_params=_cparams("parallel"),
    )(a_first, a_second, w, x)


POOL_HIST_ROWS = 16
CONV_HIST_ROWS = 32


def _cp_mixer_kernel(u_ref, hp_ref, hc_ref, pw_ref, ps_ref, cw_ref, cb_ref, lg_ref, lb_ref,
                     mix_ref, npool_ref, nconv_ref, extp_ref, extc_ref, conv_ref, shift_ref,
                     *, tt, width, start_pos, conv_k, row_chunk, col_chunk):
    ti = pl.program_id(1)
    n_t = pl.num_programs(1)
    ph, ch = POOL_HIST_ROWS, CONV_HIST_ROWS
    w = width
    gd = w // len(POOL_WINDOWS)

    @pl.when(ti == 0)
    def _():
        extp_ref[0:ph, :] = hp_ref[0]
        extc_ref[0:ch, :] = hc_ref[0]

    @pl.when(ti > 0)
    def _():
        extp_ref[0:ph, :] = extp_ref[tt:tt + ph, :]
        extc_ref[0:ch, :] = extc_ref[tt:tt + ch, :]

    extp_ref[ph:ph + tt, :] = u_ref[:, 0:w].astype(F32)
    ua = u_ref[:, w:2 * w].astype(F32)
    ub = u_ref[:, 2 * w:3 * w].astype(F32)
    extc_ref[ch:ch + tt, :] = ua * jax.nn.sigmoid(ub)

    for r0 in range(0, tt, row_chunk):
        pos = (start_pos + 1 + r0 + ti * tt
               + lax.broadcasted_iota(jnp.int32, (row_chunk, 1), 0)).astype(F32)
        for g, win in enumerate(POOL_WINDOWS):
            c0 = g * gd
            cur = extp_ref[ph + r0:ph + r0 + row_chunk, c0:c0 + gd]
            acc = cur
            for j in range(1, win):
                acc = acc + extp_ref[ph + r0 - j:ph + r0 - j + row_chunk, c0:c0 + gd]
            pooled = acc / jnp.minimum(pos, float(win)) - cur
            po = jnp.dot(pooled.astype(BF16), pw_ref[g], preferred_element_type=F32)
            mix_ref[r0:r0 + row_chunk, c0:c0 + gd] = (po * ps_ref[:, c0:c0 + gd]).astype(BF16)

    base = ch - (conv_k - 1)
    n_sh = shift_ref.shape[1]
    for s in range(1, SUBLANES):
        shift_ref[s - 1] = extc_ref[s:s + n_sh, :]

    def tap_rows(k, r0, c0):
        off = base + k
        s = off % SUBLANES
        a = off - s + r0
        if s == 0:
            return extc_ref[a:a + row_chunk, c0:c0 + col_chunk]
        return shift_ref[s - 1, a:a + row_chunk, c0:c0 + col_chunk]

    for r0 in range(0, tt, row_chunk):
        for c0 in range(0, w, col_chunk):
            acc = jnp.broadcast_to(cb_ref[:, c0:c0 + col_chunk], (row_chunk, col_chunk))
            for k in range(conv_k):
                acc = acc + cw_ref[k:k + 1, c0:c0 + col_chunk] * tap_rows(k, r0, c0)
            conv_ref[r0:r0 + row_chunk, c0:c0 + col_chunk] = acc

    for r0 in range(0, tt, row_chunk):
        c = conv_ref[r0:r0 + row_chunk, :]
        mu = jnp.mean(c, axis=-1, keepdims=True)
        cc = c - mu
        var = jnp.mean(cc * cc, axis=-1, keepdims=True)
        y = cc * lax.rsqrt(var + NORM_EPS) * lg_ref[...] + lb_ref[...]
        mix_ref[r0:r0 + row_chunk, w:2 * w] = (y * jax.nn.sigmoid(y)).astype(BF16)

    @pl.when(ti == n_t - 1)
    def _():
        npool_ref[0] = extp_ref[ph + tt - (ph - 1):ph + tt, :]
        nconv_ref[0] = extc_ref[ch + tt - (conv_k - 1):ch + tt, :]


def cp_mixer(u, row0, b, t, hist_pool, hist_conv, pool_w, pool_scale, conv_w, conv_b, ln_g, ln_b, start_pos):
    w = u.shape[1] // 3
    conv_k = conv_w.shape[0]
    n_pool_hist = hist_pool.shape[1]
    assert n_pool_hist == POOL_HIST_ROWS - 1 and conv_k - 1 <= CONV_HIST_ROWS
    tt = _pick(t, (256, 128, 64, 32, 16))
    assert row0 % tt == 0
    n_t = t // tt
    blk0 = row0 // tt
    hp = jnp.pad(hist_pool, ((0, 0), (POOL_HIST_ROWS - n_pool_hist, 0), (0, 0)))
    hc = jnp.pad(hist_conv, ((0, 0), (CONV_HIST_ROWS - (conv_k - 1), 0), (0, 0)))
    kern = functools.partial(_cp_mixer_kernel, tt=tt, width=w, start_pos=start_pos, conv_k=conv_k,
                             row_chunk=min(tt, 64), col_chunk=2 * LANES)
    n_g, gd = pool_w.shape[0], pool_w.shape[1]
    vec = lambda: pl.BlockSpec((1, w), lambda i, j: (0, 0))
    return pl.pallas_call(
        kern, name="cp_mixer",
        grid=(b, n_t),
        in_specs=[
            pl.BlockSpec((tt, 3 * w), lambda i, j: (blk0 + i * n_t + j, 0)),
            pl.BlockSpec((1, POOL_HIST_ROWS, w), lambda i, j: (i, 0, 0)),
            pl.BlockSpec((1, CONV_HIST_ROWS, w), lambda i, j: (i, 0, 0)),
            pl.BlockSpec((n_g, gd, gd), lambda i, j: (0, 0, 0)),
            vec(),
            pl.BlockSpec((conv_k, w), lambda i, j: (0, 0)),
            vec(), vec(), vec(),
        ],
        out_specs=[
            pl.BlockSpec((tt, 2 * w), lambda i, j: (i * n_t + j, 0)),
            pl.BlockSpec((1, n_pool_hist, w), lambda i, j: (i, 0, 0)),
            pl.BlockSpec((1, conv_k - 1, w), lambda i, j: (i, 0, 0)),
        ],
        out_shape=[
            jax.ShapeDtypeStruct((b * t, 2 * w), BF16),
            jax.ShapeDtypeStruct((b, n_pool_hist, w), F32),
            jax.ShapeDtypeStruct((b, conv_k - 1, w), F32),
        ],
        scratch_shapes=[
            pltpu.VMEM((POOL_HIST_ROWS + tt, w), F32),
            pltpu.VMEM((CONV_HIST_ROWS + tt, w), F32),
            pltpu.VMEM((tt, w), F32),
            pltpu.VMEM((SUBLANES - 1, CONV_HIST_ROWS + tt - SUBLANES, w), F32),
        ],
        compiler_params=_cparams("arbitrary", "arbitrary"),
    )(u, hp, hc, pool_w.astype(BF16), pool_scale.reshape(1, w), conv_w, conv_b.reshape(1, w),
      ln_g.reshape(1, w), ln_b.reshape(1, w))


FLASH_HEADS_PER_STEP = 2
FLASH_BLOCK = 512


def _flash_kernel(q_ref, k_ref, v_ref, lf_ref, o_ref, vt_ref, frow_ref, fcol_ref, *, t, n_t, head_dim, n_hp):
    s_len = t * n_t
    hd = head_dim
    n_c = s_len // LANES
    upper = (lax.broadcasted_iota(jnp.int32, (LANES, LANES), 0)
             <= lax.broadcasted_iota(jnp.int32, (LANES, LANES), 1)).astype(F32)
    earlier = (lax.broadcasted_iota(jnp.int32, (n_c, n_c), 1)
               < lax.broadcasted_iota(jnp.int32, (n_c, n_c), 0)).astype(F32)
    for hh in range(n_hp):
        for c in range(0, s_len, t):
            vt_ref[hh, :, c:c + t] = jnp.transpose(v_ref[c:c + t, hh * hd:(hh + 1) * hd].astype(F32)).astype(BF16)
        within = jnp.dot(lf_ref[0, hh], upper, preferred_element_type=F32, precision=lax.Precision.HIGHEST)
        totals = jnp.broadcast_to(within[:, LANES - 1:LANES], (n_c, LANES))
        f_all = (within + jnp.dot(earlier, totals, preferred_element_type=F32,
                                  precision=lax.Precision.HIGHEST)) * LOG2E
        frow_ref[hh] = f_all
        for c in range(n_c):
            fcol_ref[hh, c * LANES:(c + 1) * LANES, :] = jnp.transpose(
                jnp.broadcast_to(f_all[c:c + 1, :], (LANES, LANES)))

    below_diag = (lax.broadcasted_iota(jnp.int32, (t, t), 0) <= lax.broadcasted_iota(jnp.int32, (t, t), 1))

    def q_block(i, _):
        q0 = pl.multiple_of(i * t, t)
        qs = [q_ref[pl.ds(q0, t), hh * hd:(hh + 1) * hd] for hh in range(n_hp)]
        c0 = i * (t // LANES)
        fqs = [jnp.concatenate([frow_ref[hh, pl.ds(c0 + r, 1), :] for r in range(t // LANES)], axis=1)
               for hh in range(n_hp)]

        def step(j, carry, masked):
            k0 = pl.multiple_of(j * t, t)
            out = []
            for hh in range(n_hp):
                m, l, acc = carry[hh]
                k = k_ref[pl.ds(k0, t), hh * hd:(hh + 1) * hd]
                f_k = fcol_ref[hh, pl.ds(k0, t), :]
                s = lax.dot_general(k, qs[hh], _NT, preferred_element_type=F32)
                s = s + (fqs[hh] - jnp.tile(f_k, (1, t // LANES)))
                if masked:
                    s = jnp.where(below_diag, s, NEG_INF)
                m_new = jnp.maximum(m, jnp.max(s, axis=0, keepdims=True))
                alpha = jnp.exp2(m - m_new)
                p = jnp.exp2(s - m_new)
                l_new = alpha * l + jnp.sum(p, axis=0, keepdims=True)
                pv = jnp.dot(vt_ref[hh, :, pl.ds(k0, t)], p.astype(BF16), preferred_element_type=F32)
                out.append((m_new, l_new, alpha * acc + pv))
            return tuple(out)

        init = tuple((jnp.full((1, t), NEG_INF, F32), jnp.zeros((1, t), F32), jnp.zeros((hd, t), F32))
                     for _ in range(n_hp))
        carry = lax.fori_loop(0, i, lambda j, c: step(j, c, False), init)
        carry = step(i, carry, True)
        for hh in range(n_hp):
            _, l, acc = carry[hh]
            o_ref[pl.ds(q0, t), hh * hd:(hh + 1) * hd] = jnp.transpose(acc / l).astype(o_ref.dtype)
        return 0

    lax.fori_loop(0, n_t, q_block, 0)


def flash_prompt(q, k, v, logf, b, s_len, *, n_heads, head_dim):
    n_hp = FLASH_HEADS_PER_STEP
    t = _pick(s_len, (FLASH_BLOCK, 256, 128))
    n_c = s_len // LANES
    kern = functools.partial(_flash_kernel, t=t, n_t=s_len // t, head_dim=head_dim, n_hp=n_hp)
    heads = lambda: pl.BlockSpec((s_len, n_hp * head_dim), lambda i, h: (i, h))
    return pl.pallas_call(
        kern, name="flash_prompt",
        grid=(b, n_heads // n_hp),
        in_specs=[heads(), heads(), heads(), pl.BlockSpec((1, n_hp, n_c, LANES), lambda i, h: (i, h, 0, 0))],
        out_specs=heads(),
        out_shape=jax.ShapeDtypeStruct((b * s_len, n_heads * head_dim), BF16),
        scratch_shapes=[pltpu.VMEM((n_hp, head_dim, s_len), BF16), pltpu.VMEM((n_hp, n_c, LANES), F32),
                        pltpu.VMEM((n_hp, s_len, LANES), F32)],
        compiler_params=_cparams("parallel", "parallel"),
    )(q, k, v, logf)


def _sample_attn_kernel(q_ref, kn_ref, vn_ref, kc_ref, vc_ref, lf_ref, o_ref,
                        qbd_ref, m_ref, l_ref, acc_ref, kmat_ref, vmat_ref, fs_ref, *, t, n_heads, head_dim, past):
    j = pl.program_id(1)
    tp = kmat_ref.shape[0]

    @pl.when(j == 0)
    def _():
        upper = (lax.broadcasted_iota(jnp.int32, (LANES, LANES), 0)
                 <= lax.broadcasted_iota(jnp.int32, (LANES, LANES), 1)).astype(F32)
        carry = jnp.zeros((n_heads, 1), F32)
        for c0 in range(0, fs_ref.shape[1], LANES):
            f_c = carry + jnp.dot(lf_ref[0, :, c0:c0 + LANES], upper, preferred_element_type=F32,
                                  precision=lax.Precision.HIGHEST)
            fs_ref[:, c0:c0 + LANES] = f_c
            carry = f_c[:, LANES - 1:LANES]

    rows = 64

    def relayout(c, _):
        r0 = pl.multiple_of(c * rows, rows)
        for h in range(n_heads):
            kmat_ref[pl.ds(r0, rows), h * head_dim:(h + 1) * head_dim] = kc_ref[0, 0, pl.ds(r0, rows), h, :]
            vmat_ref[pl.ds(r0, rows), h * head_dim:(h + 1) * head_dim] = vc_ref[0, 0, pl.ds(r0, rows), h, :]
        return 0

    lax.fori_loop(0, kmat_ref.shape[0] // rows, relayout, 0)
    n_j = pl.num_programs(1)
    r = n_heads * t
    hd = n_heads * head_dim

    def same_head():
        return (lax.broadcasted_iota(jnp.int32, (r, hd), 0) // t
                == lax.broadcasted_iota(jnp.int32, (r, hd), 1) // head_dim)

    def rep_rows(x):
        return jnp.broadcast_to(x[:, None, :], (n_heads, t, x.shape[-1])).reshape(r, x.shape[-1])

    f_new = rep_rows(fs_ref[:, past:past + t]) * LOG2E
    row_t = lax.broadcasted_iota(jnp.int32, (r, t), 0) % t
    col_t = lax.broadcasted_iota(jnp.int32, (r, t), 1)
    f_q = jnp.sum(jnp.where(col_t == row_t, f_new, 0.0), axis=-1, keepdims=True)

    @pl.when(j == 0)
    def _():
        q_rep = jnp.broadcast_to(q_ref[...][None], (n_heads, t, hd)).reshape(r, hd)
        qbd_ref[...] = jnp.where(same_head(), q_rep, jnp.zeros_like(q_rep))
        m_ref[...] = jnp.full(m_ref.shape, NEG_INF, F32)
        l_ref[...] = jnp.zeros(l_ref.shape, F32)
        acc_ref[...] = jnp.zeros(acc_ref.shape, F32)

    def update(k, v, bias, valid):
        s = lax.dot_general(qbd_ref[...], k, _NT, preferred_element_type=F32) + bias
        if valid is not None:
            s = jnp.where(valid, s, NEG_INF)
        m = m_ref[...]
        m_new = jnp.maximum(m, jnp.max(s, axis=-1, keepdims=True))
        alpha = jnp.exp2(m - m_new)
        p = jnp.exp2(s - m_new)
        l_ref[...] = alpha * l_ref[...] + jnp.sum(p, axis=-1, keepdims=True)
        acc_ref[...] = alpha * acc_ref[...] + jnp.dot(p.astype(BF16), v, preferred_element_type=F32)
        m_ref[...] = m_new

    f_past = fs_ref[:, pl.ds(pl.multiple_of(j * tp, tp), tp)]
    update(kmat_ref[...].astype(BF16), vmat_ref[...].astype(BF16), f_q - rep_rows(f_past) * LOG2E, None)

    @pl.when(j == n_j - 1)
    def _():
        update(kn_ref[...], vn_ref[...], f_q - f_new, col_t <= row_t)
        o_full = jnp.where(same_head(), acc_ref[...] / l_ref[...], 0.0)
        o = o_full[0:t]
        for h in range(1, n_heads):
            o = o + o_full[h * t:(h + 1) * t]
        o_ref[...] = o.astype(o_ref.dtype)


def sample_attention(q, k, v, row0, b, t, cache_k, cache_v, layer, logf, *, n_heads, head_dim):
    hd = n_heads * head_dim
    p = cache_k.shape[2]
    tp = _pick(p, (512, 256, 128))
    r = n_heads * t
    assert row0 % t == 0
    blk0 = row0 // t
    n_pos = -(-(p + t) // LANES) * LANES
    logf = jnp.pad(logf, ((0, 0), (0, 0), (0, n_pos - (p + t))))
    kern = functools.partial(_sample_attn_kernel, t=t, n_heads=n_heads, head_dim=head_dim, past=p)
    new = lambda: pl.BlockSpec((t, hd), lambda i, j: (blk0 + i, 0))
    past = lambda: pl.BlockSpec((1, 1, tp, n_heads, head_dim), lambda i, j: (layer, i, j, 0, 0))
    return pl.pallas_call(
        kern, name="sample_attention",
        grid=(b, p // tp),
        in_specs=[new(), new(), new(), past(), past(),
                  pl.BlockSpec((1, n_heads, n_pos), lambda i, j: (i, 0, 0))],
        out_specs=pl.BlockSpec((t, hd), lambda i, j: (i, 0)),
        out_shape=jax.ShapeDtypeStruct((b * t, hd), BF16),
        scratch_shapes=[pltpu.VMEM((r, hd), BF16), pltpu.VMEM((r, 1), F32), pltpu.VMEM((r, 1), F32),
                        pltpu.VMEM((r, hd), F32), pltpu.VMEM((tp, hd), F32), pltpu.VMEM((tp, hd), F32),
                        pltpu.VMEM((n_heads, n_pos), F32)],
        compiler_params=_cparams("parallel", "arbitrary"),
    )(q, k, v, cache_k, cache_v, logf)


def _router_kernel(x_ref, g_ref, w_ref, b_ref, ids_ref, wts_ref, cnt_ref, seen_ref, whi_ref, wlo_ref,
                   *, n_groups, per_group):
    @pl.when(pl.program_id(0) == 0)
    def _():
        seen_ref[...] = jnp.zeros(seen_ref.shape, F32)
        w_hi = w_ref[...].astype(BF16)
        whi_ref[...] = w_hi
        wlo_ref[...] = (w_ref[...] - w_hi.astype(F32)).astype(BF16)

    h = _rms_rows(x_ref[...], g_ref[...])
    h_hi = h.astype(BF16)
    h_lo = (h - h_hi.astype(F32)).astype(BF16)
    logits = (jnp.dot(h_hi, whi_ref[...], preferred_element_type=F32)
              + jnp.dot(h_lo, whi_ref[...], preferred_element_type=F32)
              + jnp.dot(h_hi, wlo_ref[...], preferred_element_type=F32)) + b_ref[...]
    tm, n = logits.shape
    col = lax.broadcasted_iota(jnp.int32, (tm, n), 1)
    neg = jnp.float32(-jnp.inf)

    def first_max(vals):
        top = jnp.max(vals, axis=-1, keepdims=True)
        idx = jnp.min(jnp.where(vals == top, col, n), axis=-1, keepdims=True)
        return top, idx

    gl = jnp.where(col < n_groups, logits, neg)
    g_top, g_sel = first_max(gl)
    g_prob = 1.0 / jnp.sum(jnp.exp(gl - g_top), axis=-1, keepdims=True)
    lo = n_groups + g_sel * per_group
    el = jnp.where((col >= lo) & (col < lo + per_group), logits, neg)
    v1, i1 = first_max(el)
    v2, i2 = first_max(jnp.where(col == i1, neg, el))
    e2 = jnp.exp(v2 - v1)
    w1 = g_prob / (1.0 + e2)
    w2 = g_prob * e2 / (1.0 + e2)
    e1 = i1 - n_groups
    e2 = i2 - n_groups
    chosen = jnp.where((col == e1) | (col == e2), 1.0, 0.0)
    earlier_rows = (lax.broadcasted_iota(jnp.int32, (tm, tm), 1)
                    < lax.broadcasted_iota(jnp.int32, (tm, tm), 0))
    before = seen_ref[...] + jnp.dot(jnp.where(earlier_rows, 1.0, 0.0).astype(BF16), chosen.astype(BF16),
                                     preferred_element_type=F32)
    r1 = jnp.sum(jnp.where(col == e1, before, 0.0), axis=-1, keepdims=True).astype(jnp.int32)
    r2 = jnp.sum(jnp.where(col == e2, before, 0.0), axis=-1, keepdims=True).astype(jnp.int32)
    seen_ref[...] = seen_ref[...] + jnp.sum(chosen, axis=0, keepdims=True)
    cnt_ref[...] = seen_ref[...].astype(jnp.int32)
    ids_ref[...] = jnp.where(col == 0, e1, jnp.where(col == 1, e2, jnp.where(col == 2, r1,
                                                                             jnp.where(col == 3, r2, 0))))
    wts_ref[...] = jnp.where(col == 0, w1, jnp.where(col == 1, w2, 0.0))


def router(x, g, w_pad, b_pad, *, n_groups, per_group):
    m, d = x.shape
    n = w_pad.shape[1]
    tm = _pick(m, (256, 128, 64, 32, 16, 8))
    kern = functools.partial(_router_kernel, n_groups=n_groups, per_group=per_group)
    return pl.pallas_call(
        kern, name="router",
        grid=(m // tm,),
        in_specs=[_rows(tm, d), _resident((1, d)), _resident((d, n)), _resident((1, n))],
        out_specs=[_rows(tm, n), _rows(tm, n), _resident((1, n))],
        out_shape=[jax.ShapeDtypeStruct((m, n), jnp.int32), jax.ShapeDtypeStruct((m, n), F32),
                   jax.ShapeDtypeStruct((1, n), jnp.int32)],
        scratch_shapes=[pltpu.VMEM((1, n), F32), pltpu.VMEM((d, n), BF16), pltpu.VMEM((d, n), BF16)],
        compiler_params=_cparams("arbitrary"),
    )(x, g.reshape(1, d), w_pad, b_pad.reshape(1, n))


EXPERT_BLOCK_ROWS = 256


def _row_copy(src_ref, src_row, dst_ref, dst_row, sem):
    return pltpu.make_async_copy(src_ref.at[pl.ds(src_row, 1)], dst_ref.at[pl.ds(dst_row, 1)], sem)


SEL_FIELDS = 2 * TOP_K
DMA_UNROLL = 8


def _dest_row(start_ref, sel_ref, r, k):
    return start_ref[sel_ref[0, 0, SEL_FIELDS * r + k]] + sel_ref[0, 0, SEL_FIELDS * r + TOP_K + k]


def _sel_blocks(ids, tm):
    m = ids.shape[0]
    return ids[:, :SEL_FIELDS].reshape(m // tm, 1, tm * SEL_FIELDS)


def _dispatch_kernel(start_ref, sel_ref, x_ref, buf_in_ref, buf_ref, sem, *, tm):
    del buf_in_ref

    def copies(r):
        return [_row_copy(x_ref, r, buf_ref, _dest_row(start_ref, sel_ref, r, k), sem) for k in range(TOP_K)]

    def start(r, _):
        for c in copies(r):
            c.start()
        return 0

    def wait(r, _):
        for c in copies(r):
            c.wait()
        return 0

    lax.fori_loop(0, tm, start, 0, unroll=DMA_UNROLL)
    lax.fori_loop(0, tm, wait, 0, unroll=DMA_UNROLL)


def dispatch_rows(x, ids, pad_start, buf):
    m, d = x.shape
    tm = _pick(m, (256, 128, 64, 32, 16, 8))
    return pl.pallas_call(
        functools.partial(_dispatch_kernel, tm=tm), name="moe_dispatch",
        grid_spec=pltpu.PrefetchScalarGridSpec(
            num_scalar_prefetch=1,
            grid=(m // tm,),
            in_specs=[pl.BlockSpec((1, 1, tm * SEL_FIELDS), lambda i, st: (i, 0, 0), memory_space=pltpu.SMEM),
                      pl.BlockSpec((tm, d), lambda i, st: (i, 0)),
                      pl.BlockSpec(memory_space=pl.ANY)],
            out_specs=pl.BlockSpec(memory_space=pl.ANY),
            scratch_shapes=[pltpu.SemaphoreType.DMA(())],
        ),
        out_shape=jax.ShapeDtypeStruct(buf.shape, buf.dtype),
        input_output_aliases={3: 0},
        compiler_params=_cparams("arbitrary"),
    )(pad_start, _sel_blocks(ids, tm), x, buf)


def _expert_kernel(be_ref, nu_ref, x_ref, g_ref, wg_ref, wu_ref, wd_ref, o_ref, wg_b, wu_b, wd_b):
    i = pl.program_id(0)
    used = i < nu_ref[0]

    @pl.when(used & ((i == 0) | (be_ref[i] != be_ref[jnp.maximum(i - 1, 0)])))
    def _():
        wg_b[...] = wg_ref[0, 0].astype(BF16)
        wu_b[...] = wu_ref[0, 0].astype(BF16)
        wd_b[...] = wd_ref[0, 0].astype(BF16)

    @pl.when(used)
    def _():
        h = _rms_rows(x_ref[...], g_ref[...]).astype(BF16)
        gate = jnp.dot(h, wg_b[...], preferred_element_type=F32)
        up = jnp.dot(h, wu_b[...], preferred_element_type=F32)
        act = (gate * jax.nn.sigmoid(gate) * up).astype(BF16)
        o_ref[...] = jnp.dot(act, wd_b[...], preferred_element_type=F32)

    @pl.when(jnp.logical_not(used))
    def _():
        o_ref[...] = jnp.zeros(o_ref.shape, o_ref.dtype)


def expert_mlps(buf, g, w_gate, w_up, w_down, layer, block_expert, n_used):
    rows, d = buf.shape
    de = w_gate.shape[3]
    br = EXPERT_BLOCK_ROWS
    nb = rows // br
    blk = lambda i, be, nu: (jnp.minimum(i, nu[0] - 1), 0)
    wsel = lambda i, be, nu: (layer, be[i], 0, 0)
    return pl.pallas_call(
        _expert_kernel, name="expert_mlps",
        grid_spec=pltpu.PrefetchScalarGridSpec(
            num_scalar_prefetch=2,
            grid=(nb,),
            in_specs=[pl.BlockSpec((br, d), blk),
                      pl.BlockSpec((1, d), lambda i, be, nu: (0, 0)),
                      pl.BlockSpec((1, 1, d, de), wsel),
                      pl.BlockSpec((1, 1, d, de), wsel),
                      pl.BlockSpec((1, 1, de, d), wsel)],
            out_specs=pl.BlockSpec((br, d), lambda i, be, nu: (i, 0)),
            scratch_shapes=[pltpu.VMEM((d, de), BF16), pltpu.VMEM((d, de), BF16), pltpu.VMEM((de, d), BF16)],
        ),
        out_shape=jax.ShapeDtypeStruct((rows, d), F32),
        compiler_params=_cparams("arbitrary"),
    )(block_expert, n_used, buf, g.reshape(1, d), w_gate, w_up, w_down)


def _combine_ple_kernel(start_ref, sel_cur_ref, sel_nxt_ref, x_ref, wts_ref, y_ref, g_ref, w_ref, pa_ref, pb_ref,
                        wp_ref, o_ref, rows_ref, sems, xn_ref, h_ref, p_ref, *, tm, n_first):
    i = pl.program_id(0)
    n = pl.num_programs(0)

    def copies(sel_ref, slot, r):
        return [_row_copy(y_ref, _dest_row(start_ref, sel_ref, r, k), rows_ref.at[slot, k], r, sems.at[slot])
                for k in range(TOP_K)]

    def start_all(sel_ref, slot):
        def body(r, _):
            for c in copies(sel_ref, slot, r):
                c.start()
            return 0
        lax.fori_loop(0, tm, body, 0, unroll=DMA_UNROLL)

    @pl.when(i == 0)
    def _():
        start_all(sel_cur_ref, 0)

    slot = i % 2
    nxt = 1 - slot

    def wait_all(sel_ref, which):
        def wait_body(r, _):
            for c in copies(sel_ref, which, r):
                c.wait()
            return 0
        lax.fori_loop(0, tm, wait_body, 0, unroll=DMA_UNROLL)

    wait_all(sel_cur_ref, slot)

    wts = wts_ref[...]
    xn_ref[...] = x_ref[...] + wts[:, 0:1] * rows_ref[slot, 0] + wts[:, 1:2] * rows_ref[slot, 1]
    h_ref[...] = _rms_rows(xn_ref[...], g_ref[...]).astype(BF16)
    p_ref[...] = jnp.where(i < n_first, pa_ref[...], pb_ref[...]).astype(BF16)
    n_chunks = o_ref.shape[1] // COL_CHUNK
    rows_per_chunk = tm // n_chunks
    for ci in range(n_chunks):
        cs = slice(ci * COL_CHUNK, (ci + 1) * COL_CHUNK)
        for r in range(ci * rows_per_chunk, (ci + 1) * rows_per_chunk):
            for c in copies(sel_nxt_ref, nxt, r):
                c.start()
        gate = jax.nn.sigmoid(jnp.dot(h_ref[...], w_ref[:, cs], preferred_element_type=F32))
        proj = jnp.dot(p_ref[...], wp_ref[:, cs], preferred_element_type=F32)
        o_ref[:, cs] = xn_ref[:, cs] + gate * proj

    @pl.when(i == n - 1)
    def _():
        wait_all(sel_nxt_ref, nxt)


def combine_ple(x, wts, y, ids, pad_start, g, w, p_first, p_second, wp):
    m, d = x.shape
    pd = wp.shape[0]
    tm = _pick(math.gcd(p_first.shape[0], p_second.shape[0]), (256, 128, 64, 32, 16, 8))
    n_t = m // tm
    n_first = p_first.shape[0] // tm
    sel = _sel_blocks(ids, tm)
    smem = lambda f: pl.BlockSpec((1, 1, tm * SEL_FIELDS), f, memory_space=pltpu.SMEM)
    const = lambda shape: pl.BlockSpec(shape, lambda i, st: (0,) * len(shape))
    return pl.pallas_call(
        functools.partial(_combine_ple_kernel, tm=tm, n_first=n_first), name="moe_combine_ple",
        grid_spec=pltpu.PrefetchScalarGridSpec(
            num_scalar_prefetch=1,
            grid=(n_t,),
            in_specs=[smem(lambda i, st: (i, 0, 0)),
                      smem(lambda i, st: (jnp.minimum(i + 1, n_t - 1), 0, 0)),
                      pl.BlockSpec((tm, d), lambda i, st: (i, 0)),
                      pl.BlockSpec((tm, wts.shape[1]), lambda i, st: (i, 0)),
                      pl.BlockSpec(memory_space=pl.ANY),
                      const((1, d)), const((d, d)),
                      pl.BlockSpec((tm, pd), lambda i, st: (jnp.minimum(i, n_first - 1), 0)),
                      pl.BlockSpec((tm, pd), lambda i, st: (jnp.maximum(i - n_first, 0), 0)),
                      const((pd, d))],
            out_specs=pl.BlockSpec((tm, d), lambda i, st: (i, 0)),
            scratch_shapes=[pltpu.VMEM((2, TOP_K, tm, d), F32), pltpu.SemaphoreType.DMA((2,)),
                            pltpu.VMEM((tm, d), F32), pltpu.VMEM((tm, d), BF16), pltpu.VMEM((tm, pd), BF16)],
        ),
        out_shape=jax.ShapeDtypeStruct((m, d), F32),
        compiler_params=_cparams("arbitrary"),
    )(pad_start, sel, sel, x, wts, y, g.reshape(1, d), w, p_first, p_second, wp)


def _block_tables(counts, block_rows, n_blocks):
    n_experts = counts.shape[0]
    padded = (counts + block_rows - 1) // block_rows * block_rows
    pad_end = jnp.cumsum(padded)
    pad_start = (pad_end - padded).astype(jnp.int32)
    block_first_row = jnp.arange(n_blocks, dtype=jnp.int32) * block_rows
    block_expert = jnp.minimum(jnp.sum(pad_end[None, :] <= block_first_row[:, None], axis=1),
                               n_experts - 1).astype(jnp.int32)
    n_used = (pad_end[-1:] // block_rows).astype(jnp.int32)
    return pad_start, block_expert, n_used


def moe_buffer_rows(m, n_experts):
    return (-(-m * TOP_K // EXPERT_BLOCK_ROWS) + n_experts) * EXPERT_BLOCK_ROWS


def hier_moe(x, buf, g, w_router, b_router, w_gate, w_up, w_down, layer, *, n_groups, per_group):
    n_experts = n_groups * per_group
    ids, wts, counts = router(x, g, w_router, b_router, n_groups=n_groups, per_group=per_group)
    pad_start, block_expert, n_used = _block_tables(counts[0, :n_experts], EXPERT_BLOCK_ROWS,
                                                    buf.shape[0] // EXPERT_BLOCK_ROWS)
    buf = dispatch_rows(x, ids, pad_start, buf)
    y = expert_mlps(buf, g, w_gate, w_up, w_down, layer, block_expert, n_used)
    return (wts, y, ids, pad_start), buf


def _pad_cols(w, n):
    return jnp.pad(w, ((0, 0), (0, n - w.shape[1])))


def kernel(x_prompt, x_sample, cache_k, cache_v, cache_logf, state_pool, state_conv, p_prompt, p_sample,
           norm_mix, norm_ffn, norm_ple, w_in_cp, pool_w, pool_scale, conv_w, conv_b, conv_ln_g, conv_ln_b,
           w_out_cp, w_in_fox, b_forget, q_norm, k_norm, w_out_fox, router_group_w, router_group_b,
           router_expert_w, router_expert_b, expert_w_gate, expert_w_up, expert_w_down, w_ple_gate, w_ple_proj):
    bp, sp, d = x_prompt.shape
    bs, ts, _ = x_sample.shape
    depth = norm_mix.shape[0]
    n_fox = w_in_fox.shape[0]
    n_heads = b_forget.shape[1]
    head_dim = q_norm.shape[1]
    att = n_heads * head_dim
    past = cache_k.shape[2]
    n_groups = router_group_w.shape[2]
    n_experts = router_expert_w.shape[2]
    per_group = n_experts // n_groups
    width = pool_w.shape[1] * pool_w.shape[2]
    q_scale = float(head_dim) ** -0.5 * LOG2E

    mp, ms = bp * sp, bs * ts
    tm = _row_tile(mp, ms)
    x = jnp.concatenate([x_prompt.reshape(mp, d), x_sample.reshape(ms, d)], axis=0)
    pools_p, convs_p, pools_s, convs_s = [], [], [], []
    fox_out = None
    moe_buf = jnp.zeros((moe_buffer_rows(mp + ms, n_experts), d), F32)

    for i in range(depth):
        j = i // 2
        if i % 2 == 0:
            u = cp_in_proj(x, norm_mix[i], w_in_cp[j].astype(BF16), tm)
            mixer = functools.partial(cp_mixer, pool_w=pool_w[j], pool_scale=pool_scale[j], conv_w=conv_w[j],
                                      conv_b=conv_b[j], ln_g=conv_ln_g[j], ln_b=conv_ln_b[j])
            mix_p, n_pool, n_conv = mixer(u, 0, bp, sp, jnp.zeros((bp,) + state_pool.shape[2:], F32),
                                          jnp.zeros((bp,) + state_conv.shape[2:], F32), start_pos=0)
            pools_p.append(n_pool)
            convs_p.append(n_conv)
            mix_s, n_pool, n_conv = mixer(u, mp, bs, ts, state_pool[j], state_conv[j], start_pos=past)
            pools_s.append(n_pool)
            convs_s.append(n_conv)
            x = out_proj_residual(mix_p, mix_s, w_out_cp[j].astype(BF16), x, tm)
        else:
            w_in = w_in_fox[j]
            g = norm_mix[i]
            q = q_proj(x, g, w_in[:, :att].astype(BF16), q_norm[j], q_scale, tm)
            w_kvf = _pad_cols(w_in[:, att:], 2 * att + LANES).astype(BF16)
            res = kv_proj(x, g, w_kvf, k_norm[j], b_forget[j], j, n_fox, mp, fox_out, min(tm, 256))
            k16, v16 = res[0], res[1]
            fox_out = res[2:]
            lf_p = fox_out[2][j].reshape(bp, sp, n_heads)
            lf_s = fox_out[5][j].reshape(bs, ts, n_heads)
            o_p = flash_prompt(q, k16, v16, lf_p.transpose(0, 2, 1).reshape(bp, n_heads, sp // LANES, LANES),
                               bp, sp, n_heads=n_heads, head_dim=head_dim)
            lf_all = jnp.concatenate([cache_logf[j], lf_s], axis=1).transpose(0, 2, 1)
            o_s = sample_attention(q, k16, v16, mp, bs, ts, cache_k, cache_v, j, lf_all,
                                   n_heads=n_heads, head_dim=head_dim)
            x = out_proj_residual(o_p, o_s, w_out_fox[j].astype(BF16), x, tm)

        w_router = _pad_cols(jnp.concatenate([router_group_w[i], router_expert_w[i]], axis=1), LANES)
        b_router = jnp.pad(jnp.concatenate([router_group_b[i], router_expert_b[i]]),
                           (0, LANES - n_groups - n_experts))
        routed, moe_buf = hier_moe(x, moe_buf, norm_ffn[i], w_router, b_router, expert_w_gate, expert_w_up,
                                   expert_w_down, i, n_groups=n_groups, per_group=per_group)
        x = combine_ple(x, *routed, norm_ple[i], w_ple_gate[i].astype(BF16), p_prompt[i].reshape(mp, -1),
                        p_sample[i].reshape(ms, -1), w_ple_proj[i].astype(BF16))

    k4p, v4p, lfp, k4s, v4s, lfs = fox_out
    return (x[:mp].reshape(bp, sp, d), x[mp:].reshape(bs, ts, d),
            k4p.reshape(n_fox, bp, sp, n_heads, head_dim), v4p.reshape(n_fox, bp, sp, n_heads, head_dim),
            lfp.reshape(n_fox, bp, sp, n_heads), jnp.stack(pools_p), jnp.stack(convs_p),
            k4s.reshape(n_fox, bs, ts, n_heads, head_dim), v4s.reshape(n_fox, bs, ts, n_heads, head_dim),
            lfs.reshape(n_fox, bs, ts, n_heads), jnp.stack(pools_s), jnp.stack(convs_s))
```
